```python
import math
import jax, jax.numpy as jnp
from jax import lax
import numpy as np

D_MODEL = 1024
BATCH = 2
SEQ = 8192
DEPTH = 4
DEC_BATCH = 4
DEC_SEQ = 4096
PAST_LEN = 128

N_META = 16
N_MIXERS = 3
D_FF = 2816
EPS = 1e-6
CHUNK = 64
PAD = CHUNK - N_META
Q_BLOCK = 128
CONV_K = 5
ROPE_THETA = 500000.0

A_HEADS = 8
A_DK = 128
A_DV = 128
A_KW = A_HEADS * A_DK
A_VW = A_HEADS * A_DV
B_HEADS = 8
B_HD = 64
B_ROT = B_HD // 4
B_QKW = B_HEADS * B_HD
B_VW = B_HEADS * 2 * B_HD
C_HEADS = 8
C_DK = 128
C_DV = 128
C_KW = C_HEADS * C_DK
C_VW = C_HEADS * C_DV

N_A = (DEPTH + 2) // 3
N_B = (DEPTH + 1) // 3
N_C = DEPTH // 3

kernel_name = "hybrid_bidir_deltanet_diffattn_hgrn2_encoder"

F32 = jnp.float32


def rmsnorm(x, w):
    xf = x.astype(F32)
    y = xf * lax.rsqrt(jnp.mean(xf * xf, axis=-1, keepdims=True) + EPS)
    return (y * w.astype(F32)).astype(x.dtype)


def l2norm(x):
    xf = x.astype(F32)
    return xf * lax.rsqrt(jnp.sum(xf * xf, axis=-1, keepdims=True) + EPS)


def swiglu(x, w_up, w_down):
    gate, up = jnp.split(x @ w_up, 2, axis=-1)
    return (jax.nn.silu(gate) * up) @ w_down


def centred_conv(x, w):
    half = (CONV_K - 1) // 2
    return lax.conv_general_dilated(
        x, w[:, None, :].astype(x.dtype), window_strides=(1,), padding=[(half, half)],
        dimension_numbers=("NWC", "WIO", "NWC"), feature_group_count=x.shape[-1])


def rope_partial(x, pos):
    half = B_ROT // 2
    inv_freq = jnp.exp(-math.log(ROPE_THETA) * jnp.arange(half, dtype=F32) / half)
    ang = pos.astype(F32)[:, None] * inv_freq
    shape = (ang.shape[0],) + (1,) * (x.ndim - 3) + (half,)
    cos, sin = jnp.cos(ang).reshape(shape), jnp.sin(ang).reshape(shape)
    xr = x[..., :B_ROT].astype(F32)
    x1, x2 = xr[..., :half], xr[..., half:]
    rot = jnp.concatenate([x1 * cos - x2 * sin, x2 * cos + x1 * sin], axis=-1).astype(x.dtype)
    return jnp.concatenate([rot, x[..., B_ROT:]], axis=-1)


def pad_front(t):
    return jnp.pad(t, [(0, 0), (PAD, 0)] + [(0, 0)] * (t.ndim - 2))


def to_bidir(t):
    t = pad_front(t)
    return jnp.concatenate([t, jnp.flip(t, axis=1)], axis=0)


def to_bidir_dir(t):
    t = pad_front(t)
    return jnp.concatenate([t[:, :, 0], jnp.flip(t[:, :, 1], axis=1)], axis=0)


def from_bidir(o, bsz):
    return (o[:bsz] + jnp.flip(o[bsz:], axis=1))[:, PAD:]


def to_chunks(t, n_chunks):
    t = t.reshape((t.shape[0], n_chunks, CHUNK, t.shape[2]) + t.shape[3:])
    return jnp.moveaxis(t, 3, 1)


def delta_rule_chunked(q, k, v, beta, log_a):
    bsz, seq_len, n_heads, dk = q.shape
    dv = v.shape[-1]
    n_chunks = seq_len // CHUNK
    q, k, v, beta, log_a = (to_chunks(t, n_chunks) for t in (q, k, v, beta, log_a))
    gc = jnp.cumsum(log_a, axis=-1)
    incl = jnp.tril(jnp.ones((CHUNK, CHUNK), dtype=bool))
    strict = jnp.tril(jnp.ones((CHUNK, CHUNK), dtype=bool), -1)
    decay = jnp.exp(jnp.where(incl, gc[..., :, None] - gc[..., None, :], -jnp.inf))
    kb = k * beta[..., None]
    a_mat = jnp.where(strict, jnp.einsum("bhnid,bhnjd->bhnij", kb, k) * decay, 0.0)
    rhs = jnp.concatenate([v * beta[..., None], kb * jnp.exp(gc)[..., None]], axis=-1)
    sol = lax.linalg.triangular_solve(a_mat + jnp.eye(CHUNK, dtype=F32), rhs,
                                      left_side=True, lower=True, unit_diagonal=True)
    u, w = sol[..., :dv], sol[..., dv:]
    qk = jnp.einsum("bhnid,bhnjd->bhnij", q, k) * decay
    q_dec = q * jnp.exp(gc)[..., None]
    k_dec = k * jnp.exp(gc[..., -1:] - gc)[..., None]
    a_last = jnp.exp(gc[..., -1])

    def step(state, xs):
        u_c, w_c, qk_c, qd_c, kd_c, al_c = xs
        v_new = u_c - jnp.einsum("bhcd,bhde->bhce", w_c, state)
        out = jnp.einsum("bhcd,bhde->bhce", qd_c, state) + jnp.einsum("bhij,bhje->bhie", qk_c, v_new)
        state = state * al_c[..., None, None] + jnp.einsum("bhcd,bhce->bhde", kd_c, v_new)
        return state, out

    xs = tuple(jnp.moveaxis(t, 2, 0) for t in (u, w, qk, q_dec, k_dec, a_last))
    state0 = jnp.zeros((bsz, n_heads, dk, dv), F32)
    _, o = lax.scan(step, state0, xs)
    return jnp.transpose(o, (1, 0, 3, 2, 4)).reshape(bsz, seq_len, n_heads, dv)


def gla_chunked(q, k, v, log_f):
    bsz, seq_len, n_heads, dk = q.shape
    dv = v.shape[-1]
    n_chunks = seq_len // CHUNK
    q, k, v, log_f = (to_chunks(t, n_chunks) for t in (q, k, v, log_f))
    bc = jnp.cumsum(log_f, axis=-2)
    q_dec = q * jnp.exp(bc)
    k_dec = k * jnp.exp(bc[..., -1:, :] - bc)
    f_last = jnp.exp(bc[..., -1, :])
    incl = jnp.tril(jnp.ones((CHUNK, CHUNK), dtype=bool))[:, :, None]

    def step(state, xs):
        qc, kc, vc, bcc, qd, kd, fl = xs
        dec = jnp.exp(jnp.where(incl, bcc[:, :, :, None, :] - bcc[:, :, None, :, :], -jnp.inf))
        attn = jnp.einsum("bhid,bhjd,bhijd->bhij", qc, kc, dec)
        out = jnp.einsum("bhcd,bhde->bhce", qd, state) + jnp.einsum("bhij,bhje->bhie", attn, vc)
        state = state * fl[..., None] + jnp.einsum("bhcd,bhce->bhde", kd, vc)
        return state, out

    xs = tuple(jnp.moveaxis(t, 2, 0) for t in (q, k, v, bc, q_dec, k_dec, f_last))
    state0 = jnp.zeros((bsz, n_heads, dk, dv), F32)
    _, o = lax.scan(step, state0, xs)
    return jnp.transpose(o, (1, 0, 3, 2, 4)).reshape(bsz, seq_len, n_heads, dv)


def gated_deltanet(h, w_in, conv_w, a_log, dt_bias, o_norm, w_out):
    bsz, seq_len, _ = h.shape
    proj = h @ w_in
    qkv = proj[..., :2 * A_KW + A_VW]
    gate = proj[..., 2 * A_KW + A_VW:2 * A_KW + 2 * A_VW].reshape(bsz, seq_len, A_HEADS, A_DV)
    ba = proj[..., 2 * A_KW + 2 * A_VW:].astype(F32).reshape(bsz, seq_len, 2, 2, A_HEADS)
    qkv = jax.nn.silu(centred_conv(qkv, conv_w))
    q = l2norm(qkv[..., :A_KW].reshape(bsz, seq_len, A_HEADS, A_DK)) * (A_DK ** -0.5)
    k = l2norm(qkv[..., A_KW:2 * A_KW].reshape(bsz, seq_len, A_HEADS, A_DK))
    v = qkv[..., 2 * A_KW:].astype(F32).reshape(bsz, seq_len, A_HEADS, A_DV)
    beta = jax.nn.sigmoid(ba[:, :, 0])
    log_a = -jnp.exp(a_log.astype(F32)) * jax.nn.softplus(ba[:, :, 1] + dt_bias.astype(F32))
    o = delta_rule_chunked(to_bidir(q), to_bidir(k), to_bidir(v), to_bidir_dir(beta), to_bidir_dir(log_a))
    o = from_bidir(o, bsz)
    o = rmsnorm(o, o_norm) * jax.nn.silu(gate.astype(F32))
    return o.reshape(bsz, seq_len, A_VW).astype(h.dtype) @ w_out


def diff_attention(h, w_in, lam_vec, sub_norm, w_out, layer_idx):
    bsz, seq_len, _ = h.shape
    proj = h @ w_in
    qk = proj[..., :4 * B_QKW].reshape(bsz, seq_len, 4, B_HEADS, B_HD)
    v = proj[..., 4 * B_QKW:].reshape(bsz, seq_len, B_HEADS, 2 * B_HD)
    qk = rope_partial(qk, jnp.arange(seq_len))
    q = qk[:, :, :2] * (B_HD ** -0.5)
    k = qk[:, :, 2:]
    lam_init = 0.8 - 0.6 * math.exp(-0.3 * layer_idx)
    lv = lam_vec.astype(F32)
    lam = jnp.exp(jnp.sum(lv[0] * lv[1])) - jnp.exp(jnp.sum(lv[2] * lv[3])) + lam_init
    n_blk = -(-seq_len // Q_BLOCK)
    q = jnp.pad(q, ((0, 0), (0, n_blk * Q_BLOCK - seq_len), (0, 0), (0, 0), (0, 0)))
    q = jnp.moveaxis(q.reshape(bsz, n_blk, Q_BLOCK, 2, B_HEADS, B_HD), 1, 0)

    def attend(qb):
        s = jnp.einsum("bqmhd,bkmhd->bmhqk", qb, k, preferred_element_type=F32)
        p = jax.nn.softmax(s, axis=-1)
        pd = p[:, 0] - lam * p[:, 1]
        return jnp.einsum("bhqk,bkhe->bqhe", pd.astype(v.dtype), v)

    o = lax.map(attend, q)
    o = jnp.moveaxis(o, 0, 1).reshape(bsz, n_blk * Q_BLOCK, B_HEADS, 2 * B_HD)[:, :seq_len]
    o = rmsnorm(o, sub_norm) * (1.0 - lam_init)
    return o.reshape(bsz, seq_len, B_VW) @ w_out


def hgrn2(h, w_in, lb_logits, layer_idx, o_norm, w_out):
    bsz, seq_len, _ = h.shape
    proj = h @ w_in
    q = jax.nn.silu(proj[..., :C_KW].astype(F32)).reshape(bsz, seq_len, C_HEADS, C_DK) * (C_DK ** -0.5)
    i_in = proj[..., C_KW:C_KW + C_VW].astype(F32).reshape(bsz, seq_len, C_HEADS, C_DV)
    gate = proj[..., C_KW + C_VW:C_KW + 2 * C_VW].reshape(bsz, seq_len, C_HEADS, C_DV)
    f_logit = proj[..., C_KW + 2 * C_VW:].astype(F32).reshape(bsz, seq_len, 2, C_HEADS, C_DK)
    lb_w = jax.nn.softmax(lb_logits.astype(F32), axis=0)
    lb = (jnp.cumsum(lb_w, axis=0) - lb_w[0])[layer_idx].reshape(C_HEADS, C_DK)
    f = lb + (1.0 - lb) * jax.nn.sigmoid(f_logit)
    o = gla_chunked(to_bidir(q), to_bidir_dir(1.0 - f), to_bidir(i_in), to_bidir_dir(jnp.log(f)))
    o = from_bidir(o, bsz)
    o = rmsnorm(o, o_norm) * jax.nn.silu(gate.astype(F32))
    return o.reshape(bsz, seq_len, C_VW).astype(h.dtype) @ w_out


def encoder_trunk(x, meta_tokens, norm_w, ffn_w_up, ffn_w_down,
                  a_w_in, a_conv_w, a_log, a_dt_bias, a_o_norm, a_w_out,
                  b_w_in, b_lambda, b_sub_norm, b_w_out,
                  c_w_in, c_lb_logits, c_o_norm, c_w_out, final_norm):
    bsz = x.shape[0]
    meta = jnp.broadcast_to(meta_tokens.astype(x.dtype)[None], (bsz, N_META, x.shape[-1]))
    h = jnp.concatenate([meta, x], axis=1)
    for i in range(DEPTH):
        kind, j = i % N_MIXERS, i // N_MIXERS
        h = h + 0.5 * swiglu(rmsnorm(h, norm_w[i, 0]), ffn_w_up[i, 0], ffn_w_down[i, 0])
        hn = rmsnorm(h, norm_w[i, 1])
        if kind == 0:
            mix = gated_deltanet(hn, a_w_in[j], a_conv_w[j], a_log[j], a_dt_bias[j], a_o_norm[j], a_w_out[j])
        elif kind == 1:
            mix = diff_attention(hn, b_w_in[j], b_lambda[j], b_sub_norm[j], b_w_out[j], i)
        else:
            mix = hgrn2(hn, c_w_in[j], c_lb_logits, i, c_o_norm[j], c_w_out[j])
        h = h + mix.astype(h.dtype)
        h = h + 0.5 * swiglu(rmsnorm(h, norm_w[i, 2]), ffn_w_up[i, 1], ffn_w_down[i, 1])
    return rmsnorm(h, final_norm)[:, N_META:]


def setup_inputs(seed: int = 0) -> dict:
    key = jax.random.key(seed)
    ks = jax.random.split(key, 21)

    def normal(k, shape, scale):
        return jax.random.normal(k, shape, F32) * scale

    def gain(k, shape):
        return 1.0 + 0.05 * jax.random.normal(k, shape, F32)

    dt = jnp.exp(jax.random.uniform(ks[8], (N_A, 2, A_HEADS), F32, math.log(1e-3), math.log(1e-1)))
    return {
        "x_prompt": normal(ks[0], (BATCH, SEQ, D_MODEL), 1.0),
        "x_sample": normal(ks[1], (DEC_BATCH, DEC_SEQ, D_MODEL), 1.0),
        "meta_tokens": normal(ks[2], (N_META, D_MODEL), 1.0),
        "norm_w": gain(ks[3], (DEPTH, 3, D_MODEL)),
        "ffn_w_up": normal(ks[4], (DEPTH, 2, D_MODEL, 2 * D_FF), D_MODEL ** -0.5),
        "ffn_w_down": normal(ks[5], (DEPTH, 2, D_FF, D_MODEL), D_FF ** -0.5),
        "a_w_in": normal(ks[6], (N_A, D_MODEL, 2 * A_KW + 2 * A_VW + 4 * A_HEADS), D_MODEL ** -0.5),
        "a_conv_w": normal(ks[7], (N_A, CONV_K, 2 * A_KW + A_VW), CONV_K ** -0.5),
        "a_log": jnp.log(jax.random.uniform(ks[9], (N_A, 2, A_HEADS), F32, 1.0, 16.0)),
        "a_dt_bias": dt + jnp.log(-jnp.expm1(-dt)),
        "a_o_norm": gain(ks[10], (N_A, A_DV)),
        "a_w_out": normal(ks[11], (N_A, A_VW, D_MODEL), A_VW ** -0.5),
        "b_w_in": normal(ks[12], (N_B, D_MODEL, 4 * B_QKW + B_VW), D_MODEL ** -0.5),
        "b_lambda": normal(ks[13], (N_B, 4, B_HD), 0.1),
        "b_sub_norm": gain(ks[14], (N_B, 2 * B_HD)),
        "b_w_out": normal(ks[15], (N_B, B_VW, D_MODEL), B_VW ** -0.5),
        "c_w_in": normal(ks[16], (N_C, D_MODEL, 3 * C_KW + 2 * C_VW), D_MODEL ** -0.5),
        "c_lb_logits": normal(ks[17], (DEPTH, C_KW), 0.5),
        "c_o_norm": gain(ks[18], (N_C, C_DV)),
        "c_w_out": normal(ks[19], (N_C, C_VW, D_MODEL), C_VW ** -0.5),
        "final_norm": gain(ks[20], (D_MODEL,)),
    }


def reference(x_prompt, x_sample, meta_tokens, norm_w, ffn_w_up, ffn_w_down,
              a_w_in, a_conv_w, a_log, a_dt_bias, a_o_norm, a_w_out,
              b_w_in, b_lambda, b_sub_norm, b_w_out,
              c_w_in, c_lb_logits, c_o_norm, c_w_out, final_norm):
    y_prompt = encoder_trunk(x_prompt, meta_tokens, norm_w, ffn_w_up, ffn_w_down,
                             a_w_in, a_conv_w, a_log, a_dt_bias, a_o_norm, a_w_out,
                             b_w_in, b_lambda, b_sub_norm, b_w_out,
                             c_w_in, c_lb_logits, c_o_norm, c_w_out, final_norm)
    y_sample = encoder_trunk(x_sample, meta_tokens, norm_w, ffn_w_up, ffn_w_down,
                             a_w_in, a_conv_w, a_log, a_dt_bias, a_o_norm, a_w_out,
                             b_w_in, b_lambda, b_sub_norm, b_w_out,
                             c_w_in, c_lb_logits, c_o_norm, c_w_out, final_norm)
    return (y_prompt, y_sample)
```

```python
import functools
import math

import numpy as np
import jax
import jax.numpy as jnp
from jax import lax
from jax.experimental import pallas as pl
from jax.experimental.pallas import tpu as pltpu

F32 = jnp.float32
BF16 = jnp.bfloat16

EPS = 1e-6
CHUNK = 64
N_META = 16
PAD = CHUNK - N_META
N_MIXERS = 3
CONV_K = 5
ROPE_THETA = 500000.0
HEADS = 8
HEAD_DIM = 128
ATT_HD = 64
ATT_ROT = ATT_HD // 4
LANES = 128
V7X_VMEM_LIMIT_BYTES = 56 * 1024 * 1024

ROW_TILE = 512
CONV_TILE = 448
ATT_TQ = 256
ATT_TK = 256


def _params(*sem):
    return pltpu.CompilerParams(dimension_semantics=sem, vmem_limit_bytes=V7X_VMEM_LIMIT_BYTES)


def _dot(a, b):
    return jnp.dot(a, b, preferred_element_type=F32)


def _dot_nt(a, b):
    return lax.dot_general(a, b, (((1,), (1,)), ((), ())), preferred_element_type=F32)


def _dot_tn(a, b):
    return lax.dot_general(a, b, (((0,), (0,)), ((), ())), preferred_element_type=F32)


def _split3(x):
    hi = x.astype(BF16)
    r = x - hi.astype(F32)
    mid = r.astype(BF16)
    lo = (r - mid.astype(F32)).astype(BF16)
    return hi, mid, lo


def _mask_dot(mask, x):
    hi, mid, lo = _split3(x)
    return _dot(mask, hi) + _dot(mask, mid) + _dot(mask, lo)


def _dot_mask_nt(x, mask):
    hi, mid, lo = _split3(x)
    return _dot_nt(hi, mask) + _dot_nt(mid, mask) + _dot_nt(lo, mask)


def _rms(x, w):
    return x * lax.rsqrt(jnp.mean(x * x, axis=-1, keepdims=True) + EPS) * w


def _silu(x):
    return x * jax.nn.sigmoid(x)


def _softplus(x):
    return jnp.maximum(x, 0.0) + jnp.log(1.0 + jnp.exp(-jnp.abs(x)))


def _resident(shape):
    nd = len(shape)
    return pl.BlockSpec(shape, lambda *_: (0,) * nd, pipeline_mode=pl.Buffered(1))


def _ffn_kernel(x_ref, nw_ref, wup_ref, wdn_ref, fn_ref, o_ref, *, d_ff, n_split, final):
    x = x_ref[...]
    xn = _rms(x, nw_ref[...]).astype(BF16)
    step = d_ff // n_split
    y = None
    for c in range(n_split):
        g = _dot(xn, wup_ref[:, c * step:(c + 1) * step])
        u = _dot(xn, wup_ref[:, d_ff + c * step:d_ff + (c + 1) * step])
        a = (_silu(g) * u).astype(BF16)
        part = _dot(a, wdn_ref[c * step:(c + 1) * step, :])
        y = part if y is None else y + part
    h = x + 0.5 * y
    if final:
        h = _rms(h, fn_ref[...])
    o_ref[...] = h


def _ffn(h, nw, w_up, w_down, final_w, final):
    t, d = h.shape
    d_ff = w_down.shape[0]
    n_split = 2 if (d_ff // 2) % LANES == 0 else 1
    row = pl.BlockSpec((ROW_TILE, d), lambda i: (i, 0))
    return pl.pallas_call(
        functools.partial(_ffn_kernel, d_ff=d_ff, n_split=n_split, final=final),
        grid=(pl.cdiv(t, ROW_TILE),),
        in_specs=[row, _resident((1, d)), _resident(w_up.shape), _resident(w_down.shape),
                  _resident((1, d))],
        out_specs=row,
        out_shape=jax.ShapeDtypeStruct((t, d), F32),
        compiler_params=_params("parallel"),
        name="ffn",
    )(h, nw.reshape(1, d), w_up, w_down, final_w.reshape(1, d))


def _proj_kernel(x_ref, nw_ref, *refs):
    n = len(refs) // 2
    xn = _rms(x_ref[...], nw_ref[...]).astype(BF16)
    for w_ref, o_ref in zip(refs[:n], refs[n:]):
        o_ref[...] = _dot(xn, w_ref[...])


def _norm_proj(h, nw, weights):
    t, d = h.shape
    row = pl.BlockSpec((ROW_TILE, d), lambda i: (i, 0))
    return pl.pallas_call(
        _proj_kernel,
        grid=(pl.cdiv(t, ROW_TILE),),
        in_specs=[row, _resident((1, d))] + [_resident(w.shape) for w in weights],
        out_specs=[pl.BlockSpec((ROW_TILE, w.shape[1]), lambda i: (i, 0)) for w in weights],
        out_shape=[jax.ShapeDtypeStruct((t, w.shape[1]), F32) for w in weights],
        compiler_params=_params("parallel"),
        name="norm_proj",
    )(h, nw.reshape(1, d), *weights)


def _out_kernel(h_ref, *refs, n_o, gated, scale):
    o_refs = refs[:n_o]
    rest = refs[n_o:]
    if gated:
        gate_ref, on_ref, w_ref, out_ref = rest
    else:
        on_ref, w_ref, out_ref = rest
    o = o_refs[0][...]
    for r in o_refs[1:]:
        o = o + r[...]
    on = on_ref[...]
    parts = []
    for hd in range(HEADS):
        sl = slice(hd * HEAD_DIM, (hd + 1) * HEAD_DIM)
        y = _rms(o[:, sl], on)
        if gated:
            y = y * _silu(gate_ref[:, sl])
        else:
            y = y * scale
        parts.append(y.astype(BF16))
    y = jnp.concatenate(parts, axis=-1)
    out_ref[...] = h_ref[...] + _dot(y, w_ref[...])


def _out_proj(h, o_list, gate, gate_block, o_norm, w_out, scale=1.0):
    t, d = h.shape
    row = pl.BlockSpec((ROW_TILE, d), lambda i: (i, 0))
    in_specs = [row]
    args = [h]
    for arr, lead in o_list:
        if lead is None:
            in_specs.append(row)
        else:
            in_specs.append(pl.BlockSpec((None, ROW_TILE, d), lambda i, lead=lead: (lead, i, 0)))
        args.append(arr)
    gated = gate is not None
    if gated:
        in_specs.append(pl.BlockSpec((ROW_TILE, d), lambda i: (i, gate_block)))
        args.append(gate)
    in_specs += [_resident((1, HEAD_DIM)), _resident(w_out.shape)]
    args += [o_norm.reshape(1, HEAD_DIM), w_out]
    return pl.pallas_call(
        functools.partial(_out_kernel, n_o=len(o_list), gated=gated, scale=scale),
        grid=(pl.cdiv(t, ROW_TILE),),
        in_specs=in_specs,
        out_specs=row,
        out_shape=jax.ShapeDtypeStruct((t, d), F32),
        compiler_params=_params("parallel"),
        name="out_proj",
    )(*args)


def _conv_kernel(first_ref, xm_ref, xp_ref, xn_ref, w_ref, o_ref, buf_ref, *, tile):
    i = pl.program_id(0)
    j = pl.program_id(1)
    last = pl.num_programs(0) - 1
    buf_ref[8:8 + tile, :] = xm_ref[...]
    buf_ref[0:8, :] = jnp.where(i > 0, xp_ref[...], 0.0)
    buf_ref[8 + tile:16 + tile, :] = jnp.where(i < last, xn_ref[...], 0.0)
    half = (CONV_K - 1) // 2
    acc = None
    for k in range(CONV_K):
        term = w_ref[k:k + 1, :] * buf_ref[8 - half + k:8 - half + k + tile, :]
        acc = term if acc is None else acc + term
    norm_scale = jnp.where(j == 0, HEAD_DIM ** -0.5, 1.0)
    row = lax.broadcasted_iota(jnp.int32, (CHUNK, 1), 0)
    per_tile = tile // CHUNK
    for c in range(per_tile):
        n_dead = jnp.where(first_ref[i * per_tile + c] > 0, PAD, 0)
        y = jnp.where(row < n_dead, 0.0, acc[c * CHUNK:(c + 1) * CHUNK, :])
        y = _silu(y)
        for hd in range(HEADS):
            sl = slice(hd * HEAD_DIM, (hd + 1) * HEAD_DIM)
            blk = y[:, sl]
            inv = lax.rsqrt(jnp.sum(blk * blk, axis=-1, keepdims=True) + EPS) * norm_scale
            inv = jnp.where(j < 2, inv, 1.0)
            o_ref[c * CHUNK:(c + 1) * CHUNK, sl] = blk * inv


def _conv_qkv(proj, conv_w, first):
    t = proj.shape[0]
    width = HEADS * HEAD_DIM
    tile = CONV_TILE if t % CONV_TILE == 0 else CHUNK
    nb8 = t // 8
    tb8 = tile // 8
    grid_spec = pltpu.PrefetchScalarGridSpec(
        num_scalar_prefetch=1,
        grid=(t // tile, 3),
        in_specs=[
            pl.BlockSpec((tile, width), lambda i, j, f: (i, j)),
            pl.BlockSpec((8, width), lambda i, j, f: (jnp.maximum(i * tb8 - 1, 0), j)),
            pl.BlockSpec((8, width), lambda i, j, f: (jnp.minimum((i + 1) * tb8, nb8 - 1), j)),
            pl.BlockSpec((CONV_K, width), lambda i, j, f: (0, j)),
        ],
        out_specs=pl.BlockSpec((tile, width), lambda i, j, f: (i, j)),
        scratch_shapes=[pltpu.VMEM((tile + 16, width), F32)],
    )
    return pl.pallas_call(
        functools.partial(_conv_kernel, tile=tile),
        grid_spec=grid_spec,
        out_shape=jax.ShapeDtypeStruct((t, 3 * width), F32),
        compiler_params=_params("arbitrary", "arbitrary"),
        name="conv_qkv",
    )(first, proj, proj, proj, conv_w)


def _direction_masks(d):
    ii = lax.broadcasted_iota(jnp.int32, (CHUNK, CHUNK), 0)
    jj = lax.broadcasted_iota(jnp.int32, (CHUNK, CHUNK), 1)
    diff = (ii - jj) * jnp.where(d == 0, 1, -1)
    return ii, jj, diff >= 0, diff > 0


def _delta_kernel(first_ref, last_ref, q_ref, k_ref, v_ref, bac_ref, bar_ref,
                  alog_r_ref, dtb_r_ref, alog_c_ref, dtb_c_ref, o_ref, s_ref):
    d = pl.program_id(0)
    s = pl.program_id(1)
    nc = pl.num_programs(1)
    c = jnp.where(d == 0, s, nc - 1 - s)
    reset = jnp.where(d == 0, first_ref[c], last_ref[c])

    @pl.when(reset > 0)
    def _():
        s_ref[...] = jnp.zeros_like(s_ref)

    ii, jj, incl, strict = _direction_masks(d)
    m_incl = jnp.where(incl, 1.0, 0.0).astype(BF16)
    eye = jnp.where(ii == jj, 1.0, 0.0)
    n_lvl = int(math.log2(CHUNK))
    lvl = [((ii >> (b + 1)) == (jj >> (b + 1))) & ((ii >> b) != (jj >> b)) for b in range(n_lvl)]

    bac = bac_ref[...]
    bar = bar_ref[...]
    beta_c = jax.nn.sigmoid(bac[:, :HEADS])
    la_c = -jnp.exp(alog_r_ref[...]) * _softplus(bac[:, HEADS:] + dtb_r_ref[...])
    la_r = -jnp.exp(alog_c_ref[...]) * _softplus(bar[HEADS:, :] + dtb_c_ref[...])
    gc_c = _mask_dot(m_incl, la_c)
    gc_r = _dot_mask_nt(la_r, m_incl)
    gt_c = jnp.sum(la_c, axis=0, keepdims=True)

    for hd in range(HEADS):
        sl = slice(hd * HEAD_DIM, (hd + 1) * HEAD_DIM)
        qh = q_ref[:, sl]
        kh = k_ref[:, sl]
        vh = v_ref[:, sl]
        g_col = gc_c[:, hd:hd + 1]
        g_row = gc_r[hd:hd + 1, :]
        g_tot = gt_c[:, hd:hd + 1]
        b_col = beta_c[:, hd:hd + 1]
        decay = jnp.exp(jnp.where(incl, g_col - g_row, -jnp.inf))
        kb = kh * b_col
        kh16 = kh.astype(BF16)
        a_mat = jnp.where(strict, _dot_nt(kb.astype(BF16), kh16) * decay, 0.0)
        t_inv = eye - jnp.where(lvl[0], a_mat, 0.0)
        for b in range(1, n_lvl):
            t16 = t_inv.astype(BF16)
            a_off = jnp.where(lvl[b], a_mat, 0.0).astype(BF16)
            t_inv = t_inv - _dot(t16, _dot(a_off, t16).astype(BF16))
        t16 = t_inv.astype(BF16)
        e_col = jnp.exp(g_col)
        u = _dot(t16, (vh * b_col).astype(BF16))
        w = _dot(t16, (kb * e_col).astype(BF16))
        qk = _dot_nt(qh.astype(BF16), kh16) * decay
        q_dec = qh * e_col
        k_dec = kh * jnp.exp(g_tot - g_col)
        state = s_ref[hd]
        st16 = state.astype(BF16)
        v_new = u - _dot(w.astype(BF16), st16)
        vn16 = v_new.astype(BF16)
        o_ref[:, sl] = _dot(q_dec.astype(BF16), st16) + _dot(qk.astype(BF16), vn16)
        s_ref[hd] = state * jnp.exp(g_tot) + _dot_tn(k_dec.astype(BF16), vn16)


def _delta_rule(qkv, ba_c, ba_r, a_log, dt_bias, first, last):
    t = qkv.shape[0]
    nc = t // CHUNK
    width = HEADS * HEAD_DIM

    def chunk(d, s):
        return jnp.where(d == 0, s, nc - 1 - s)

    def col(j):
        return pl.BlockSpec((CHUNK, width), lambda d, s, f, l: (chunk(d, s), j))

    par_r = pl.BlockSpec((None, 1, HEADS), lambda d, s, f, l: (d, 0, 0))
    par_c = pl.BlockSpec((None, HEADS, 1), lambda d, s, f, l: (d, 0, 0))
    grid_spec = pltpu.PrefetchScalarGridSpec(
        num_scalar_prefetch=2,
        grid=(2, nc),
        in_specs=[
            col(0), col(1), col(2),
            pl.BlockSpec((None, CHUNK, 2 * HEADS), lambda d, s, f, l: (d, chunk(d, s), 0)),
            pl.BlockSpec((None, None, 2 * HEADS, CHUNK), lambda d, s, f, l: (d, chunk(d, s), 0, 0)),
            par_r, par_r, par_c, par_c,
        ],
        out_specs=pl.BlockSpec((None, CHUNK, width), lambda d, s, f, l: (d, chunk(d, s), 0)),
        scratch_shapes=[pltpu.VMEM((HEADS, HEAD_DIM, HEAD_DIM), F32)],
    )
    return pl.pallas_call(
        _delta_kernel,
        grid_spec=grid_spec,
        out_shape=jax.ShapeDtypeStruct((2, t, width), F32),
        compiler_params=_params("arbitrary", "arbitrary"),
        name="delta_rule",
    )(first, last, qkv, qkv, qkv, ba_c, ba_r,
      a_log.reshape(2, 1, HEADS), dt_bias.reshape(2, 1, HEADS),
      a_log.reshape(2, HEADS, 1), dt_bias.reshape(2, HEADS, 1))


def _gated_deltanet(h, nw, w_in, conv_w, a_log, dt_bias, o_norm, w_out, first, last):
    t = h.shape[0]
    width = HEADS * HEAD_DIM
    proj, ba = _norm_proj(h, nw, [w_in[:, :4 * width], w_in[:, 4 * width:]])
    qkv = _conv_qkv(proj, conv_w, first)
    ba = ba.reshape(t, 2, 2, HEADS)
    ba_c = jnp.transpose(ba, (2, 0, 1, 3)).reshape(2, t, 2 * HEADS)
    ba_r = jnp.transpose(ba_c.reshape(2, t // CHUNK, CHUNK, 2 * HEADS), (0, 1, 3, 2))
    o = _delta_rule(qkv, ba_c, ba_r, a_log, dt_bias, first, last)
    return _out_proj(h, [(o, 0), (o, 1)], proj, 3, o_norm, w_out)


def _gla_kernel(first_ref, last_ref, q_ref, v_ref, f_ref, lb_ref, o_ref, s_ref, *, layer_idx):
    d = pl.program_id(0)
    s = pl.program_id(1)
    nc = pl.num_programs(1)
    c = jnp.where(d == 0, s, nc - 1 - s)
    reset = jnp.where(d == 0, first_ref[c], last_ref[c])

    @pl.when(reset > 0)
    def _():
        s_ref[...] = jnp.zeros_like(s_ref)

    ii, jj, incl, strict = _direction_masks(d)
    n_lvl = int(math.log2(CHUNK))
    fwd = d == 0
    mats = [jnp.where(incl, 1.0, 0.0)]
    pair_masks = []
    for b in range(n_lvl):
        size = 1 << b
        mid = ((ii >> (b + 1)) << (b + 1)) + size
        upper = (ii & size) != 0
        same = (ii >> (b + 1)) == (jj >> (b + 1))
        f_up = (jj >= mid) & (jj <= ii)
        f_lo = (jj > ii) & (jj < mid)
        b_lo = (jj >= ii) & (jj < mid)
        b_up = (jj >= mid) & (jj < ii)
        lower = jnp.logical_not(upper)
        sel_f = jnp.where(((upper & f_up) | (lower & f_lo)) & same, 1.0, 0.0)
        sel_b = jnp.where(((upper & b_up) | (lower & b_lo)) & same, 1.0, 0.0)
        mats.append(jnp.where(fwd, sel_f, sel_b))
        pair_masks.append(same & ((ii >> b) != (jj >> b)) & strict)
    m_all = jnp.concatenate(mats, axis=0).astype(BF16)

    lbl = lb_ref[...]
    lbw = jnp.exp(lbl - jnp.max(lbl, axis=0, keepdims=True))
    lbw = lbw / jnp.sum(lbw, axis=0, keepdims=True)
    lb = jnp.zeros_like(lbw[0:1, :])
    for r in range(1, layer_idx + 1):
        lb = lb + lbw[r:r + 1, :]

    f = lb + (1.0 - lb) * jax.nn.sigmoid(f_ref[...])
    kk = 1.0 - f
    logf = jnp.log(f)
    e_all = _mask_dot(m_all, logf)
    bc = e_all[:CHUNK, :]
    tot = jnp.sum(logf, axis=0, keepdims=True)
    x_all = jnp.exp(e_all)
    q = _silu(q_ref[...]) * (HEAD_DIM ** -0.5)
    q_dec = q * x_all[:CHUNK, :]
    k_dec = kk * jnp.exp(tot - bc)
    f_last = jnp.exp(tot)

    for hd in range(HEADS):
        sl = slice(hd * HEAD_DIM, (hd + 1) * HEAD_DIM)
        qh = q[:, sl]
        kh = kk[:, sl]
        attn = jnp.where(ii == jj, _dot_nt(qh.astype(BF16), kh.astype(BF16)), 0.0)
        for b in range(n_lvl):
            xb = x_all[(b + 1) * CHUNK:(b + 2) * CHUNK, sl]
            part = _dot_nt((qh * xb).astype(BF16), (kh * xb).astype(BF16))
            attn = jnp.where(pair_masks[b], part, attn)
        vh = v_ref[:, sl]
        v16 = vh.astype(BF16)
        st_t = s_ref[hd]
        o_ref[:, sl] = _dot_nt(q_dec[:, sl].astype(BF16), st_t.astype(BF16)) + _dot(attn.astype(BF16), v16)
        s_ref[hd] = st_t * f_last[:, sl] + _dot_tn(v16, k_dec[:, sl].astype(BF16))


def _gla(proj, lb_logits, layer_idx, first, last):
    t = proj.shape[0]
    nc = t // CHUNK
    width = HEADS * HEAD_DIM

    def chunk(d, s):
        return jnp.where(d == 0, s, nc - 1 - s)

    grid_spec = pltpu.PrefetchScalarGridSpec(
        num_scalar_prefetch=2,
        grid=(2, nc),
        in_specs=[
            pl.BlockSpec((CHUNK, width), lambda d, s, f, l: (chunk(d, s), 0)),
            pl.BlockSpec((CHUNK, width), lambda d, s, f, l: (chunk(d, s), 1)),
            pl.BlockSpec((CHUNK, width), lambda d, s, f, l: (chunk(d, s), 3 + d)),
            pl.BlockSpec(lb_logits.shape, lambda d, s, f, l: (0, 0)),
        ],
        out_specs=pl.BlockSpec((None, CHUNK, width), lambda d, s, f, l: (d, chunk(d, s), 0)),
        scratch_shapes=[pltpu.VMEM((HEADS, HEAD_DIM, HEAD_DIM), F32)],
    )
    return pl.pallas_call(
        functools.partial(_gla_kernel, layer_idx=layer_idx),
        grid_spec=grid_spec,
        out_shape=jax.ShapeDtypeStruct((2, t, width), F32),
        compiler_params=_params("arbitrary", "arbitrary"),
        name="gla",
    )(first, last, proj, proj, proj, lb_logits)


def _hgrn2(h, nw, w_in, lb_logits, layer_idx, o_norm, w_out, first, last):
    (proj,) = _norm_proj(h, nw, [w_in])
    o = _gla(proj, lb_logits, layer_idx, first, last)
    return _out_proj(h, [(o, 0), (o, 1)], proj, 2, o_norm, w_out)


def _rope_kernel(x_ref, cos_ref, s1_ref, s2_ref, o_ref, *, n_rot_tiles, n_q_tiles):
    cos = cos_ref[...]
    s1 = s1_ref[...]
    s2 = s2_ref[...]
    for c in range(n_rot_tiles):
        sl = slice(c * LANES, (c + 1) * LANES)
        x = x_ref[:, sl]
        half = ATT_ROT // 2
        y = x * cos + pltpu.roll(x, LANES - half, 1) * s1 + pltpu.roll(x, half, 1) * s2
        if c < n_q_tiles:
            y = y * (ATT_HD ** -0.5)
        o_ref[:, sl] = y.astype(BF16)
    o_ref[:, n_rot_tiles * LANES:] = x_ref[:, n_rot_tiles * LANES:].astype(BF16)


def _rope(proj, cos_t, s1_t, s2_t):
    t, n = proj.shape
    qk_w = HEADS * ATT_HD
    row = pl.BlockSpec((ATT_TQ, n), lambda i: (i, 0))
    tab = pl.BlockSpec((ATT_TQ, LANES), lambda i: (i, 0))
    return pl.pallas_call(
        functools.partial(_rope_kernel, n_rot_tiles=4 * qk_w // LANES, n_q_tiles=2 * qk_w // LANES),
        grid=(pl.cdiv(t, ATT_TQ),),
        in_specs=[row, tab, tab, tab],
        out_specs=row,
        out_shape=jax.ShapeDtypeStruct((t, n), BF16),
        compiler_params=_params("parallel"),
        name="rope",
    )(proj, cos_t, s1_t, s2_t)


def _attn_kernel(q1_ref, q2_ref, k1_ref, k2_ref, v_ref, lam_ref, o_ref, acc_ref, m_ref, l_ref,
                 *, seq_len, lam_init):
    i = pl.program_id(2)
    lane = lax.broadcasted_iota(jnp.int32, (1, LANES), 1)
    q_refs = (q1_ref, q2_ref)
    k_refs = (k1_ref, k2_ref)
    qm = []
    for hh in range(2):
        keep = (lane >= hh * ATT_HD) & (lane < (hh + 1) * ATT_HD)
        qm.append([jnp.where(keep, q_refs[m][...], jnp.zeros((), BF16)) for m in range(2)])

    m_ref[...] = jnp.full(m_ref.shape, -jnp.inf, F32)
    l_ref[...] = jnp.zeros_like(l_ref)
    acc_ref[...] = jnp.zeros_like(acc_ref)

    def tile(start, size, masked):
        if masked:
            colid = lax.broadcasted_iota(jnp.int32, (1, size), 1)
        for hh in range(2):
            vt = v_ref[pl.ds(start, size), hh * HEAD_DIM:(hh + 1) * HEAD_DIM]
            for m in range(2):
                idx = hh * 2 + m
                sc = _dot_nt(qm[hh][m], k_refs[m][pl.ds(start, size), :])
                if masked:
                    sc = jnp.where(colid >= PAD, sc, -jnp.inf)
                m_old = m_ref[idx]
                m_new = jnp.maximum(m_old, jnp.max(sc, axis=-1, keepdims=True))
                alpha = jnp.exp(m_old - m_new)
                p = jnp.exp(sc - m_new)
                l_ref[idx] = alpha * l_ref[idx] + jnp.sum(p, axis=-1, keepdims=True)
                acc_ref[idx] = alpha * acc_ref[idx] + _dot(p.astype(BF16), vt)
                m_ref[idx] = m_new

    first_size = min(ATT_TK, seq_len)
    tile(0, first_size, True)
    n_full = seq_len // ATT_TK
    if n_full > 1:
        def body(j, carry):
            tile(pl.multiple_of(j * ATT_TK, ATT_TK), ATT_TK, False)
            return carry
        lax.fori_loop(1, n_full, body, 0)
    tail = seq_len - max(n_full, 1) * ATT_TK
    if tail > 0:
        tile(n_full * ATT_TK, tail, False)

    lv = lam_ref[...]
    lam = (jnp.exp(jnp.sum(lv[0:1] * lv[1:2], axis=-1, keepdims=True))
           - jnp.exp(jnp.sum(lv[2:3] * lv[3:4], axis=-1, keepdims=True)) + lam_init)
    row = i * ATT_TQ + lax.broadcasted_iota(jnp.int32, (ATT_TQ, 1), 0)
    for hh in range(2):
        o = acc_ref[2 * hh] / l_ref[2 * hh] - lam * (acc_ref[2 * hh + 1] / l_ref[2 * hh + 1])
        o_ref[:, hh * HEAD_DIM:(hh + 1) * HEAD_DIM] = jnp.where(row >= PAD, o, 0.0)


def _attention(qkv, lam_vec, lam_init):
    b, seq_len, _ = qkv.shape
    width = HEADS * HEAD_DIM
    n_pair = HEADS // 2
    pair_w = 2 * ATT_HD
    nq = pl.cdiv(seq_len, ATT_TQ)

    def qspec(off):
        return pl.BlockSpec((None, ATT_TQ, pair_w), lambda bb, p, i: (bb, i, off + p))

    def kspec(off):
        return pl.BlockSpec((None, seq_len, pair_w), lambda bb, p, i: (bb, 0, off + p))

    return pl.pallas_call(
        functools.partial(_attn_kernel, seq_len=seq_len, lam_init=lam_init),
        grid=(b, n_pair, nq),
        in_specs=[
            qspec(0), qspec(n_pair), kspec(2 * n_pair), kspec(3 * n_pair),
            pl.BlockSpec((None, seq_len, 2 * HEAD_DIM), lambda bb, p, i: (bb, 0, 2 * n_pair + p)),
            pl.BlockSpec(lam_vec.shape, lambda bb, p, i: (0, 0)),
        ],
        out_specs=pl.BlockSpec((None, ATT_TQ, 2 * HEAD_DIM), lambda bb, p, i: (bb, i, p)),
        out_shape=jax.ShapeDtypeStruct((b, seq_len, width), F32),
        scratch_shapes=[pltpu.VMEM((4, ATT_TQ, HEAD_DIM), F32),
                        pltpu.VMEM((4, ATT_TQ, 1), F32),
                        pltpu.VMEM((4, ATT_TQ, 1), F32)],
        compiler_params=_params("parallel", "parallel", "arbitrary"),
        name="diff_attention",
    )(qkv, qkv, qkv, qkv, qkv, lam_vec)


def _rope_tables(pos):
    half = ATT_ROT // 2
    inv_freq = jnp.exp(-math.log(ROPE_THETA) * jnp.arange(half, dtype=F32) / half)
    ang = pos.astype(F32)[:, None] * inv_freq
    cos, sin = jnp.cos(ang), jnp.sin(ang)
    n = pos.shape[0]
    rest = ATT_HD - ATT_ROT
    cos_t = jnp.concatenate([cos, cos, jnp.ones((n, rest), F32)], axis=1)
    s1_t = jnp.concatenate([-sin, jnp.zeros((n, ATT_HD - half), F32)], axis=1)
    s2_t = jnp.concatenate([jnp.zeros((n, half), F32), sin, jnp.zeros((n, rest), F32)], axis=1)
    rep = LANES // ATT_HD
    return tuple(jnp.tile(x, (1, rep)) for x in (cos_t, s1_t, s2_t))


def _diff_attention(h, nw, w_in, lam_vec, sub_norm, w_out, layer_idx, groups, pos):
    (proj,) = _norm_proj(h, nw, [w_in])
    roped = _rope(proj, *_rope_tables(pos))
    lam_init = 0.8 - 0.6 * math.exp(-0.3 * layer_idx)
    outs = []
    start = 0
    for b, seq_len in groups:
        rows = b * seq_len
        part = roped[start:start + rows].reshape(b, seq_len, roped.shape[1])
        outs.append(_attention(part, lam_vec, lam_init).reshape(rows, -1))
        start += rows
    o = jnp.concatenate(outs, axis=0)
    return _out_proj(h, [(o, None)], None, 0, sub_norm, w_out, scale=1.0 - lam_init)


def kernel(x_prompt, x_sample, meta_tokens, norm_w, ffn_w_up, ffn_w_down,
           a_w_in, a_conv_w, a_log, a_dt_bias, a_o_norm, a_w_out,
           b_w_in, b_lambda, b_sub_norm, b_w_out,
           c_w_in, c_lb_logits, c_o_norm, c_w_out, final_norm):
    d = x_prompt.shape[-1]
    depth = norm_w.shape[0]
    xs = (x_prompt, x_sample)
    groups = [(x.shape[0], x.shape[1] + CHUNK) for x in xs]

    parts = []
    for x in xs:
        b = x.shape[0]
        lead = jnp.concatenate([jnp.zeros((PAD, d), F32), meta_tokens.astype(F32)], axis=0)
        lead = jnp.broadcast_to(lead[None], (b, CHUNK, d))
        parts.append(jnp.concatenate([lead, x.astype(F32)], axis=1).reshape(-1, d))
    h = jnp.concatenate(parts, axis=0)
    t = h.shape[0]

    first = np.zeros((t // CHUNK,), np.int32)
    last = np.zeros((t // CHUNK,), np.int32)
    pos = np.zeros((t,), np.int32)
    start = 0
    for b, seq_len in groups:
        for _ in range(b):
            first[start // CHUNK] = 1
            last[(start + seq_len) // CHUNK - 1] = 1
            pos[start:start + seq_len] = np.maximum(np.arange(seq_len) - PAD, 0)
            start += seq_len
    first, last, pos = jnp.asarray(first), jnp.asarray(last), jnp.asarray(pos)

    w_up16 = ffn_w_up.astype(BF16)
    w_dn16 = ffn_w_down.astype(BF16)
    for i in range(depth):
        kind, j = i % N_MIXERS, i // N_MIXERS
        h = _ffn(h, norm_w[i, 0], w_up16[i, 0], w_dn16[i, 0], final_norm, False)
        if kind == 0:
            h = _gated_deltanet(h, norm_w[i, 1], a_w_in[j].astype(BF16), a_conv_w[j], a_log[j],
                                a_dt_bias[j], a_o_norm[j], a_w_out[j].astype(BF16), first, last)
        elif kind == 1:
            h = _diff_attention(h, norm_w[i, 1], b_w_in[j].astype(BF16), b_lambda[j], b_sub_norm[j],
                                b_w_out[j].astype(BF16), i, groups, pos)
        else:
            h = _hgrn2(h, norm_w[i, 1], c_w_in[j].astype(BF16), c_lb_logits, i, c_o_norm[j],
                       c_w_out[j].astype(BF16), first, last)
        h = _ffn(h, norm_w[i, 2], w_up16[i, 1], w_dn16[i, 1], final_norm, i == depth - 1)

    outs = []
    start = 0
    for (b, seq_len), x in zip(groups, xs):
        rows = b * seq_len
        outs.append(h[start:start + rows].reshape(b, seq_len, d)[:, CHUNK:].astype(x.dtype))
        start += rows
    return tuple(outs)
```

```python
import functools
import math

import numpy as np
import jax
import jax.numpy as jnp
from jax import lax
from jax.experimental import pallas as pl
from jax.experimental.pallas import tpu as pltpu

F32 = jnp.float32
BF16 = jnp.bfloat16

EPS = 1e-6
CHUNK = 64
N_META = 16
PAD = CHUNK - N_META
N_MIXERS = 3
CONV_K = 5
ROPE_THETA = 500000.0
HEADS = 8
HEAD_DIM = 128
ATT_HD = 64
ATT_ROT = ATT_HD // 4
LANES = 128
V7X_VMEM_LIMIT_BYTES = 56 * 1024 * 1024

ROW_TILE = 512
CONV_TILE = 448
ATT_TQ = 256
ATT_TK = 512
LOG2E = 1.4426950408889634


def _params(*sem):
    return pltpu.CompilerParams(dimension_semantics=sem, vmem_limit_bytes=V7X_VMEM_LIMIT_BYTES)


def _dot(a, b):
    return jnp.dot(a, b, preferred_element_type=F32)


def _dot_nt(a, b):
    return lax.dot_general(a, b, (((1,), (1,)), ((), ())), preferred_element_type=F32)


def _dot_tn(a, b):
    return lax.dot_general(a, b, (((0,), (0,)), ((), ())), preferred_element_type=F32)


def _split3(x):
    hi = x.astype(BF16)
    r = x - hi.astype(F32)
    mid = r.astype(BF16)
    lo = (r - mid.astype(F32)).astype(BF16)
    return hi, mid, lo


def _mask_dot(mask, x):
    hi, mid, lo = _split3(x)
    return _dot(mask, hi) + _dot(mask, mid) + _dot(mask, lo)


def _dot_mask_nt(x, mask):
    hi, mid, lo = _split3(x)
    return _dot_nt(hi, mask) + _dot_nt(mid, mask) + _dot_nt(lo, mask)


def _rms(x, w):
    return x * lax.rsqrt(jnp.mean(x * x, axis=-1, keepdims=True) + EPS) * w


def _silu(x):
    return x * jax.nn.sigmoid(x)


def _softplus(x):
    return jnp.maximum(x, 0.0) + jnp.log(1.0 + jnp.exp(-jnp.abs(x)))


def _resident(shape):
    nd = len(shape)
    return pl.BlockSpec(shape, lambda *_: (0,) * nd, pipeline_mode=pl.Buffered(1))


def _ffn_kernel(x_ref, nw_ref, wup_ref, wdn_ref, fn_ref, o_ref, *, d_ff, n_split, final):
    x = x_ref[...]
    xn = _rms(x, nw_ref[...]).astype(BF16)
    step = d_ff // n_split
    y = None
    for c in range(n_split):
        g = _dot(xn, wup_ref[:, c * step:(c + 1) * step])
        u = _dot(xn, wup_ref[:, d_ff + c * step:d_ff + (c + 1) * step])
        a = (_silu(g) * u).astype(BF16)
        part = _dot(a, wdn_ref[c * step:(c + 1) * step, :])
        y = part if y is None else y + part
    h = x + 0.5 * y
    if final:
        h = _rms(h, fn_ref[...])
    o_ref[...] = h


def _ffn(h, nw, w_up, w_down, final_w, final):
    t, d = h.shape
    d_ff = w_down.shape[0]
    n_split = 2 if (d_ff // 2) % LANES == 0 else 1
    row = pl.BlockSpec((ROW_TILE, d), lambda i: (i, 0))
    return pl.pallas_call(
        functools.partial(_ffn_kernel, d_ff=d_ff, n_split=n_split, final=final),
        grid=(pl.cdiv(t, ROW_TILE),),
        in_specs=[row, _resident((1, d)), _resident(w_up.shape), _resident(w_down.shape),
                  _resident((1, d))],
        out_specs=row,
        out_shape=jax.ShapeDtypeStruct((t, d), F32),
        compiler_params=_params("parallel"),
        name="ffn",
    )(h, nw.reshape(1, d), w_up, w_down, final_w.reshape(1, d))


def _proj_kernel(x_ref, nw_ref, *refs):
    n = len(refs) // 2
    xn = _rms(x_ref[...], nw_ref[...]).astype(BF16)
    for w_ref, o_ref in zip(refs[:n], refs[n:]):
        o_ref[...] = _dot(xn, w_ref[...])


def _norm_proj(h, nw, weights):
    t, d = h.shape
    row = pl.BlockSpec((ROW_TILE, d), lambda i: (i, 0))
    return pl.pallas_call(
        _proj_kernel,
        grid=(pl.cdiv(t, ROW_TILE),),
        in_specs=[row, _resident((1, d))] + [_resident(w.shape) for w in weights],
        out_specs=[pl.BlockSpec((ROW_TILE, w.shape[1]), lambda i: (i, 0)) for w in weights],
        out_shape=[jax.ShapeDtypeStruct((t, w.shape[1]), F32) for w in weights],
        compiler_params=_params("parallel"),
        name="norm_proj",
    )(h, nw.reshape(1, d), *weights)


def _out_kernel(h_ref, *refs, n_o, gated, scale):
    o_refs = refs[:n_o]
    rest = refs[n_o:]
    if gated:
        gate_ref, on_ref, w_ref, out_ref = rest
    else:
        on_ref, w_ref, out_ref = rest
    o = o_refs[0][...]
    for r in o_refs[1:]:
        o = o + r[...]
    on = on_ref[...]
    parts = []
    for hd in range(HEADS):
        sl = slice(hd * HEAD_DIM, (hd + 1) * HEAD_DIM)
        y = _rms(o[:, sl], on)
        if gated:
            y = y * _silu(gate_ref[:, sl])
        else:
            y = y * scale
        parts.append(y.astype(BF16))
    y = jnp.concatenate(parts, axis=-1)
    out_ref[...] = h_ref[...] + _dot(y, w_ref[...])


def _out_proj(h, o_list, gate, gate_block, o_norm, w_out, scale=1.0):
    t, d = h.shape
    row = pl.BlockSpec((ROW_TILE, d), lambda i: (i, 0))
    in_specs = [row]
    args = [h]
    for arr, lead in o_list:
        if lead is None:
            in_specs.append(row)
        else:
            in_specs.append(pl.BlockSpec((None, ROW_TILE, d), lambda i, lead=lead: (lead, i, 0)))
        args.append(arr)
    gated = gate is not None
    if gated:
        in_specs.append(pl.BlockSpec((ROW_TILE, d), lambda i: (i, gate_block)))
        args.append(gate)
    in_specs += [_resident((1, HEAD_DIM)), _resident(w_out.shape)]
    args += [o_norm.reshape(1, HEAD_DIM), w_out]
    return pl.pallas_call(
        functools.partial(_out_kernel, n_o=len(o_list), gated=gated, scale=scale),
        grid=(pl.cdiv(t, ROW_TILE),),
        in_specs=in_specs,
        out_specs=row,
        out_shape=jax.ShapeDtypeStruct((t, d), F32),
        compiler_params=_params("parallel"),
        name="out_proj",
    )(*args)


def _conv_kernel(first_ref, xm_ref, xp_ref, xn_ref, w_ref, o_ref, buf_ref, *, tile):
    i = pl.program_id(0)
    j = pl.program_id(1)
    last = pl.num_programs(0) - 1
    buf_ref[8:8 + tile, :] = xm_ref[...]
    buf_ref[0:8, :] = jnp.where(i > 0, xp_ref[...], 0.0)
    buf_ref[8 + tile:16 + tile, :] = jnp.where(i < last, xn_ref[...], 0.0)
    half = (CONV_K - 1) // 2
    acc = None
    for k in range(CONV_K):
        term = w_ref[k:k + 1, :] * buf_ref[8 - half + k:8 - half + k + tile, :]
        acc = term if acc is None else acc + term
    norm_scale = jnp.where(j == 0, HEAD_DIM ** -0.5, 1.0)
    row = lax.broadcasted_iota(jnp.int32, (CHUNK, 1), 0)
    per_tile = tile // CHUNK
    for c in range(per_tile):
        n_dead = jnp.where(first_ref[i * per_tile + c] > 0, PAD, 0)
        y = jnp.where(row < n_dead, 0.0, acc[c * CHUNK:(c + 1) * CHUNK, :])
        y = _silu(y)
        for hd in range(HEADS):
            sl = slice(hd * HEAD_DIM, (hd + 1) * HEAD_DIM)
            blk = y[:, sl]
            inv = lax.rsqrt(jnp.sum(blk * blk, axis=-1, keepdims=True) + EPS) * norm_scale
            inv = jnp.where(j < 2, inv, 1.0)
            o_ref[c * CHUNK:(c + 1) * CHUNK, sl] = blk * inv


def _conv_qkv(proj, conv_w, first):
    t = proj.shape[0]
    width = HEADS * HEAD_DIM
    tile = CONV_TILE if t % CONV_TILE == 0 else CHUNK
    nb8 = t // 8
    tb8 = tile // 8
    grid_spec = pltpu.PrefetchScalarGridSpec(
        num_scalar_prefetch=1,
        grid=(t // tile, 3),
        in_specs=[
            pl.BlockSpec((tile, width), lambda i, j, f: (i, j)),
            pl.BlockSpec((8, width), lambda i, j, f: (jnp.maximum(i * tb8 - 1, 0), j)),
            pl.BlockSpec((8, width), lambda i, j, f: (jnp.minimum((i + 1) * tb8, nb8 - 1), j)),
            pl.BlockSpec((CONV_K, width), lambda i, j, f: (0, j)),
        ],
        out_specs=pl.BlockSpec((tile, width), lambda i, j, f: (i, j)),
        scratch_shapes=[pltpu.VMEM((tile + 16, width), F32)],
    )
    return pl.pallas_call(
        functools.partial(_conv_kernel, tile=tile),
        grid_spec=grid_spec,
        out_shape=jax.ShapeDtypeStruct((t, 3 * width), F32),
        compiler_params=_params("arbitrary", "arbitrary"),
        name="conv_qkv",
    )(first, proj, proj, proj, conv_w)


def _direction_masks(d):
    ii = lax.broadcasted_iota(jnp.int32, (CHUNK, CHUNK), 0)
    jj = lax.broadcasted_iota(jnp.int32, (CHUNK, CHUNK), 1)
    diff = (ii - jj) * jnp.where(d == 0, 1, -1)
    return ii, jj, diff >= 0, diff > 0


def _delta_kernel(first_ref, last_ref, q_ref, k_ref, v_ref, bac_ref, bar_ref,
                  alog_r_ref, dtb_r_ref, alog_c_ref, dtb_c_ref, o_ref, s_ref):
    d = pl.program_id(0)
    s = pl.program_id(1)
    nc = pl.num_programs(1)
    c = jnp.where(d == 0, s, nc - 1 - s)
    reset = jnp.where(d == 0, first_ref[c], last_ref[c])

    @pl.when(reset > 0)
    def _():
        s_ref[...] = jnp.zeros_like(s_ref)

    ii, jj, incl, strict = _direction_masks(d)
    m_incl = jnp.where(incl, 1.0, 0.0).astype(BF16)
    eye = jnp.where(ii == jj, 1.0, 0.0)
    n_lvl = int(math.log2(CHUNK))
    lvl = [((ii >> (b + 1)) == (jj >> (b + 1))) & ((ii >> b) != (jj >> b)) for b in range(n_lvl)]

    bac = bac_ref[...]
    bar = bar_ref[...]
    beta_c = jax.nn.sigmoid(bac[:, :HEADS])
    la_c = -jnp.exp(alog_r_ref[...]) * _softplus(bac[:, HEADS:] + dtb_r_ref[...])
    la_r = -jnp.exp(alog_c_ref[...]) * _softplus(bar[HEADS:, :] + dtb_c_ref[...])
    gc_c = _mask_dot(m_incl, la_c)
    gc_r = _dot_mask_nt(la_r, m_incl)
    gt_c = jnp.sum(la_c, axis=0, keepdims=True)

    hs = range(HEADS)
    sls = [slice(hd * HEAD_DIM, (hd + 1) * HEAD_DIM) for hd in hs]
    qs = [q_ref[:, sl] for sl in sls]
    ks = [k_ref[:, sl] for sl in sls]
    vs = [v_ref[:, sl] for sl in sls]
    g_col = [gc_c[:, hd:hd + 1] for hd in hs]
    g_tot = [gt_c[:, hd:hd + 1] for hd in hs]
    b_col = [beta_c[:, hd:hd + 1] for hd in hs]
    decay = [jnp.exp(jnp.where(incl, g_col[hd] - gc_r[hd:hd + 1, :], -jnp.inf)) for hd in hs]
    kb = [ks[hd] * b_col[hd] for hd in hs]
    k16 = [x.astype(BF16) for x in ks]
    a_mat = [jnp.where(strict, _dot_nt(kb[hd].astype(BF16), k16[hd]) * decay[hd], 0.0) for hd in hs]
    t_inv = [eye - jnp.where(lvl[0], a, 0.0) for a in a_mat]
    for b in range(1, n_lvl):
        t16 = [t.astype(BF16) for t in t_inv]
        a_off = [jnp.where(lvl[b], a, 0.0).astype(BF16) for a in a_mat]
        x16 = [_dot(a_off[hd], t16[hd]).astype(BF16) for hd in hs]
        t_inv = [t_inv[hd] - _dot(t16[hd], x16[hd]) for hd in hs]
    t16 = [t.astype(BF16) for t in t_inv]
    e_col = [jnp.exp(g) for g in g_col]
    u = [_dot(t16[hd], (vs[hd] * b_col[hd]).astype(BF16)) for hd in hs]
    w16 = [_dot(t16[hd], (kb[hd] * e_col[hd]).astype(BF16)).astype(BF16) for hd in hs]
    qk16 = [(_dot_nt(qs[hd].astype(BF16), k16[hd]) * decay[hd]).astype(BF16) for hd in hs]
    qd16 = [(qs[hd] * e_col[hd]).astype(BF16) for hd in hs]
    kd16 = [(ks[hd] * jnp.exp(g_tot[hd] - g_col[hd])).astype(BF16) for hd in hs]
    state = [s_ref[hd] for hd in hs]
    st16 = [x.astype(BF16) for x in state]
    vn16 = [(u[hd] - _dot(w16[hd], st16[hd])).astype(BF16) for hd in hs]
    for hd in hs:
        o_ref[:, sls[hd]] = _dot(qd16[hd], st16[hd]) + _dot(qk16[hd], vn16[hd])
    for hd in hs:
        s_ref[hd] = state[hd] * jnp.exp(g_tot[hd]) + _dot_tn(kd16[hd], vn16[hd])


def _delta_rule(qkv, ba_c, ba_r, a_log, dt_bias, first, last):
    t = qkv.shape[0]
    nc = t // CHUNK
    width = HEADS * HEAD_DIM

    def chunk(d, s):
        return jnp.where(d == 0, s, nc - 1 - s)

    def col(j):
        return pl.BlockSpec((CHUNK, width), lambda d, s, f, l: (chunk(d, s), j))

    par_r = pl.BlockSpec((None, 1, HEADS), lambda d, s, f, l: (d, 0, 0))
    par_c = pl.BlockSpec((None, HEADS, 1), lambda d, s, f, l: (d, 0, 0))
    grid_spec = pltpu.PrefetchScalarGridSpec(
        num_scalar_prefetch=2,
        grid=(2, nc),
        in_specs=[
            col(0), col(1), col(2),
            pl.BlockSpec((None, CHUNK, 2 * HEADS), lambda d, s, f, l: (d, chunk(d, s), 0)),
            pl.BlockSpec((None, None, 2 * HEADS, CHUNK), lambda d, s, f, l: (d, chunk(d, s), 0, 0)),
            par_r, par_r, par_c, par_c,
        ],
        out_specs=pl.BlockSpec((None, CHUNK, width), lambda d, s, f, l: (d, chunk(d, s), 0)),
        scratch_shapes=[pltpu.VMEM((HEADS, HEAD_DIM, HEAD_DIM), F32)],
    )
    return pl.pallas_call(
        _delta_kernel,
        grid_spec=grid_spec,
        out_shape=jax.ShapeDtypeStruct((2, t, width), F32),
        compiler_params=_params("arbitrary", "arbitrary"),
        name="delta_rule",
    )(first, last, qkv, qkv, qkv, ba_c, ba_r,
      a_log.reshape(2, 1, HEADS), dt_bias.reshape(2, 1, HEADS),
      a_log.reshape(2, HEADS, 1), dt_bias.reshape(2, HEADS, 1))


def _gated_deltanet(h, nw, w_in, conv_w, a_log, dt_bias, o_norm, w_out, first, last):
    t = h.shape[0]
    width = HEADS * HEAD_DIM
    proj, ba = _norm_proj(h, nw, [w_in[:, :4 * width], w_in[:, 4 * width:]])
    qkv = _conv_qkv(proj, conv_w, first)
    ba = ba.reshape(t, 2, 2, HEADS)
    ba_c = jnp.transpose(ba, (2, 0, 1, 3)).reshape(2, t, 2 * HEADS)
    ba_r = jnp.transpose(ba_c.reshape(2, t // CHUNK, CHUNK, 2 * HEADS), (0, 1, 3, 2))
    o = _delta_rule(qkv, ba_c, ba_r, a_log, dt_bias, first, last)
    return _out_proj(h, [(o, 0), (o, 1)], proj, 3, o_norm, w_out)


def _gla_kernel(first_ref, last_ref, q_ref, v_ref, f_ref, lb_ref, o_ref, s_ref, *, layer_idx):
    d = pl.program_id(0)
    s = pl.program_id(1)
    nc = pl.num_programs(1)
    c = jnp.where(d == 0, s, nc - 1 - s)
    reset = jnp.where(d == 0, first_ref[c], last_ref[c])

    @pl.when(reset > 0)
    def _():
        s_ref[...] = jnp.zeros_like(s_ref)

    ii, jj, incl, strict = _direction_masks(d)
    n_lvl = int(math.log2(CHUNK))
    fwd = d == 0
    mats = [jnp.where(incl, 1.0, 0.0)]
    pair_masks = []
    for b in range(n_lvl):
        size = 1 << b
        mid = ((ii >> (b + 1)) << (b + 1)) + size
        upper = (ii & size) != 0
        same = (ii >> (b + 1)) == (jj >> (b + 1))
        f_up = (jj >= mid) & (jj <= ii)
        f_lo = (jj > ii) & (jj < mid)
        b_lo = (jj >= ii) & (jj < mid)
        b_up = (jj >= mid) & (jj < ii)
        lower = jnp.logical_not(upper)
        sel_f = jnp.where(((upper & f_up) | (lower & f_lo)) & same, 1.0, 0.0)
        sel_b = jnp.where(((upper & b_up) | (lower & b_lo)) & same, 1.0, 0.0)
        mats.append(jnp.where(fwd, sel_f, sel_b))
        pair_masks.append(same & ((ii >> b) != (jj >> b)) & strict)
    m_all = jnp.concatenate(mats, axis=0).astype(BF16)

    lbl = lb_ref[...]
    lbw = jnp.exp(lbl - jnp.max(lbl, axis=0, keepdims=True))
    lbw = lbw / jnp.sum(lbw, axis=0, keepdims=True)
    lb = jnp.zeros_like(lbw[0:1, :])
    for r in range(1, layer_idx + 1):
        lb = lb + lbw[r:r + 1, :]

    f = lb + (1.0 - lb) * jax.nn.sigmoid(f_ref[...])
    kk = 1.0 - f
    logf = jnp.log(f)
    e_all = _mask_dot(m_all, logf)
    bc = e_all[:CHUNK, :]
    tot = jnp.sum(logf, axis=0, keepdims=True)
    x_all = jnp.exp(e_all)
    q = _silu(q_ref[...]) * (HEAD_DIM ** -0.5)
    q_dec = q * x_all[:CHUNK, :]
    k_dec = kk * jnp.exp(tot - bc)
    f_last = jnp.exp(tot)

    hs = range(HEADS)
    sls = [slice(hd * HEAD_DIM, (hd + 1) * HEAD_DIM) for hd in hs]
    qs = [q[:, sl] for sl in sls]
    ks = [kk[:, sl] for sl in sls]
    attn = [jnp.where(ii == jj, _dot_nt(qs[hd].astype(BF16), ks[hd].astype(BF16)), 0.0) for hd in hs]
    for b in range(n_lvl):
        xb = [x_all[(b + 1) * CHUNK:(b + 2) * CHUNK, sl] for sl in sls]
        part = [_dot_nt((qs[hd] * xb[hd]).astype(BF16), (ks[hd] * xb[hd]).astype(BF16)) for hd in hs]
        attn = [jnp.where(pair_masks[b], part[hd], attn[hd]) for hd in hs]
    v16 = [v_ref[:, sl].astype(BF16) for sl in sls]
    st_t = [s_ref[hd] for hd in hs]
    for hd in hs:
        o_ref[:, sls[hd]] = (_dot_nt(q_dec[:, sls[hd]].astype(BF16), st_t[hd].astype(BF16))
                             + _dot(attn[hd].astype(BF16), v16[hd]))
    for hd in hs:
        s_ref[hd] = st_t[hd] * f_last[:, sls[hd]] + _dot_tn(v16[hd], k_dec[:, sls[hd]].astype(BF16))


def _gla(proj, lb_logits, layer_idx, first, last):
    t = proj.shape[0]
    nc = t // CHUNK
    width = HEADS * HEAD_DIM

    def chunk(d, s):
        return jnp.where(d == 0, s, nc - 1 - s)

    grid_spec = pltpu.PrefetchScalarGridSpec(
        num_scalar_prefetch=2,
        grid=(2, nc),
        in_specs=[
            pl.BlockSpec((CHUNK, width), lambda d, s, f, l: (chunk(d, s), 0)),
            pl.BlockSpec((CHUNK, width), lambda d, s, f, l: (chunk(d, s), 1)),
            pl.BlockSpec((CHUNK, width), lambda d, s, f, l: (chunk(d, s), 3 + d)),
            pl.BlockSpec(lb_logits.shape, lambda d, s, f, l: (0, 0)),
        ],
        out_specs=pl.BlockSpec((None, CHUNK, width), lambda d, s, f, l: (d, chunk(d, s), 0)),
        scratch_shapes=[pltpu.VMEM((HEADS, HEAD_DIM, HEAD_DIM), F32)],
    )
    return pl.pallas_call(
        functools.partial(_gla_kernel, layer_idx=layer_idx),
        grid_spec=grid_spec,
        out_shape=jax.ShapeDtypeStruct((2, t, width), F32),
        compiler_params=_params("arbitrary", "arbitrary"),
        name="gla",
    )(first, last, proj, proj, proj, lb_logits)


def _hgrn2(h, nw, w_in, lb_logits, layer_idx, o_norm, w_out, first, last):
    (proj,) = _norm_proj(h, nw, [w_in])
    o = _gla(proj, lb_logits, layer_idx, first, last)
    return _out_proj(h, [(o, 0), (o, 1)], proj, 2, o_norm, w_out)


def _rope_kernel(x_ref, cos_ref, s1_ref, s2_ref, o_ref, *, n_rot_tiles, n_q_tiles):
    cos = cos_ref[...]
    s1 = s1_ref[...]
    s2 = s2_ref[...]
    for c in range(n_rot_tiles):
        sl = slice(c * LANES, (c + 1) * LANES)
        x = x_ref[:, sl]
        half = ATT_ROT // 2
        y = x * cos + pltpu.roll(x, LANES - half, 1) * s1 + pltpu.roll(x, half, 1) * s2
        if c < n_q_tiles:
            y = y * (ATT_HD ** -0.5 * LOG2E)
        o_ref[:, sl] = y.astype(BF16)
    o_ref[:, n_rot_tiles * LANES:] = x_ref[:, n_rot_tiles * LANES:].astype(BF16)


def _rope(proj, cos_t, s1_t, s2_t):
    t, n = proj.shape
    qk_w = HEADS * ATT_HD
    row = pl.BlockSpec((ATT_TQ, n), lambda i: (i, 0))
    tab = pl.BlockSpec((ATT_TQ, LANES), lambda i: (i, 0))
    return pl.pallas_call(
        functools.partial(_rope_kernel, n_rot_tiles=4 * qk_w // LANES, n_q_tiles=2 * qk_w // LANES),
        grid=(pl.cdiv(t, ATT_TQ),),
        in_specs=[row, tab, tab, tab],
        out_specs=row,
        out_shape=jax.ShapeDtypeStruct((t, n), BF16),
        compiler_params=_params("parallel"),
        name="rope",
    )(proj, cos_t, s1_t, s2_t)


def _attn_kernel(q1_ref, q2_ref, k1_ref, k2_ref, v_ref, lam_ref, o_ref, vt_ref, *acc_refs,
                 seq_len, lam_init):
    i = pl.program_id(2)
    n_full = seq_len // ATT_TK
    tail = seq_len - n_full * ATT_TK

    @pl.when(i == 0)
    def _():
        def body(j, carry):
            blk = v_ref[pl.ds(pl.multiple_of(j * ATT_TK, ATT_TK), ATT_TK), :]
            vt_ref[j] = blk.astype(F32).T.astype(BF16)
            return carry
        if n_full > 0:
            lax.fori_loop(0, n_full, body, 0)
        if tail > 0:
            blk = v_ref[n_full * ATT_TK:seq_len, :]
            vt_ref[n_full, :, :tail] = blk.astype(F32).T.astype(BF16)

    lane = lax.broadcasted_iota(jnp.int32, (1, LANES), 1)
    q_refs = (q1_ref, q2_ref)
    k_refs = (k1_ref, k2_ref)
    qm = []
    for hh in range(2):
        keep = (lane >= hh * ATT_HD) & (lane < (hh + 1) * ATT_HD)
        qm.append([jnp.where(keep, q_refs[m][...], jnp.zeros((), BF16)) for m in range(2)])

    for a in acc_refs:
        a[...] = jnp.zeros_like(a)
    units = [(hh, m) for hh in range(2) for m in range(2)]

    def tile(j, start, size, masked, carry):
        m_old, l_old = carry[:4], carry[4:]
        k_t = [k_refs[m][pl.ds(start, size), :] for m in range(2)]
        if size == ATT_TK:
            vt = [vt_ref[j, hh * HEAD_DIM:(hh + 1) * HEAD_DIM, :] for hh in range(2)]
        else:
            vt = [vt_ref[j, hh * HEAD_DIM:(hh + 1) * HEAD_DIM, :size] for hh in range(2)]
        sc = [_dot_nt(k_t[m], qm[hh][m]) for hh, m in units]
        if masked:
            rowid = lax.broadcasted_iota(jnp.int32, (size, 1), 0)
            sc = [jnp.where(rowid >= PAD, s, -jnp.inf) for s in sc]
        m_new = [jnp.maximum(m_old[u], jnp.max(sc[u], axis=0, keepdims=True)) for u in range(4)]
        alpha = [jnp.exp2(m_old[u] - m_new[u]) for u in range(4)]
        p = [jnp.exp2(sc[u] - m_new[u]) for u in range(4)]
        l_new = [alpha[u] * l_old[u] + jnp.sum(p[u], axis=0, keepdims=True) for u in range(4)]
        pv = [_dot(vt[units[u][0]], p[u].astype(BF16)) for u in range(4)]
        for u in range(4):
            acc_refs[u][...] = alpha[u] * acc_refs[u][...] + pv[u]
        return tuple(m_new) + tuple(l_new)

    carry = (jnp.full((1, ATT_TQ), -jnp.inf, F32),) * 4 + (jnp.zeros((1, ATT_TQ), F32),) * 4
    if n_full > 0:
        carry = tile(0, 0, ATT_TK, True, carry)
        if n_full > 1:
            def body(j, c):
                return tile(j, pl.multiple_of(j * ATT_TK, ATT_TK), ATT_TK, False, c)
            carry = lax.fori_loop(1, n_full, body, carry)
        if tail > 0:
            carry = tile(n_full, n_full * ATT_TK, tail, False, carry)
    else:
        carry = tile(0, 0, tail, True, carry)
    l_fin = carry[4:]

    lv = lam_ref[...]
    lam = (jnp.exp(jnp.sum(lv[0:1] * lv[1:2], axis=-1, keepdims=True))
           - jnp.exp(jnp.sum(lv[2:3] * lv[3:4], axis=-1, keepdims=True)) + lam_init)
    row = i * ATT_TQ + lax.broadcasted_iota(jnp.int32, (ATT_TQ, 1), 0)
    for hh in range(2):
        o_t = (acc_refs[2 * hh][...] / l_fin[2 * hh]
               - lam * (acc_refs[2 * hh + 1][...] / l_fin[2 * hh + 1]))
        o_ref[:, hh * HEAD_DIM:(hh + 1) * HEAD_DIM] = jnp.where(row >= PAD, o_t.T, 0.0)


def _attention(qkv, lam_vec, lam_init):
    b, seq_len, _ = qkv.shape
    width = HEADS * HEAD_DIM
    n_pair = HEADS // 2
    pair_w = 2 * ATT_HD
    nq = pl.cdiv(seq_len, ATT_TQ)

    def qspec(off):
        return pl.BlockSpec((None, ATT_TQ, pair_w), lambda bb, p, i: (bb, i, off + p))

    def kspec(off):
        return pl.BlockSpec((None, seq_len, pair_w), lambda bb, p, i: (bb, 0, off + p))

    return pl.pallas_call(
        functools.partial(_attn_kernel, seq_len=seq_len, lam_init=lam_init),
        grid=(b, n_pair, nq),
        in_specs=[
            qspec(0), qspec(n_pair), kspec(2 * n_pair), kspec(3 * n_pair),
            pl.BlockSpec((None, seq_len, 2 * HEAD_DIM), lambda bb, p, i: (bb, 0, 2 * n_pair + p)),
            pl.BlockSpec(lam_vec.shape, lambda bb, p, i: (0, 0)),
        ],
        out_specs=pl.BlockSpec((None, ATT_TQ, 2 * HEAD_DIM), lambda bb, p, i: (bb, i, p)),
        out_shape=jax.ShapeDtypeStruct((b, seq_len, width), F32),
        scratch_shapes=[pltpu.VMEM((pl.cdiv(seq_len, ATT_TK), 2 * HEAD_DIM, ATT_TK), BF16)]
        + [pltpu.VMEM((HEAD_DIM, ATT_TQ), F32)] * 4,
        compiler_params=_params("arbitrary", "arbitrary", "arbitrary"),
        name="diff_attention",
    )(qkv, qkv, qkv, qkv, qkv, lam_vec)


def _rope_tables(pos):
    half = ATT_ROT // 2
    inv_freq = jnp.exp(-math.log(ROPE_THETA) * jnp.arange(half, dtype=F32) / half)
    ang = pos.astype(F32)[:, None] * inv_freq
    cos, sin = jnp.cos(ang), jnp.sin(ang)
    n = pos.shape[0]
    rest = ATT_HD - ATT_ROT
    cos_t = jnp.concatenate([cos, cos, jnp.ones((n, rest), F32)], axis=1)
    s1_t = jnp.concatenate([-sin, jnp.zeros((n, ATT_HD - half), F32)], axis=1)
    s2_t = jnp.concatenate([jnp.zeros((n, half), F32), sin, jnp.zeros((n, rest), F32)], axis=1)
    rep = LANES // ATT_HD
    return tuple(jnp.tile(x, (1, rep)) for x in (cos_t, s1_t, s2_t))


def _diff_attention(h, nw, w_in, lam_vec, sub_norm, w_out, layer_idx, groups, pos):
    (proj,) = _norm_proj(h, nw, [w_in])
    roped = _rope(proj, *_rope_tables(pos))
    lam_init = 0.8 - 0.6 * math.exp(-0.3 * layer_idx)
    outs = []
    start = 0
    for b, seq_len in groups:
        rows = b * seq_len
        part = roped[start:start + rows].reshape(b, seq_len, roped.shape[1])
        outs.append(_attention(part, lam_vec, lam_init).reshape(rows, -1))
        start += rows
    o = jnp.concatenate(outs, axis=0)
    return _out_proj(h, [(o, None)], None, 0, sub_norm, w_out, scale=1.0 - lam_init)


def kernel(x_prompt, x_sample, meta_tokens, norm_w, ffn_w_up, ffn_w_down,
           a_w_in, a_conv_w, a_log, a_dt_bias, a_o_norm, a_w_out,
           b_w_in, b_lambda, b_sub_norm, b_w_out,
           c_w_in, c_lb_logits, c_o_norm, c_w_out, final_norm):
    d = x_prompt.shape[-1]
    depth = norm_w.shape[0]
    xs = (x_prompt, x_sample)
    groups = [(x.shape[0], x.shape[1] + CHUNK) for x in xs]

    parts = []
    for x in xs:
        b = x.shape[0]
        lead = jnp.concatenate([jnp.zeros((PAD, d), F32), meta_tokens.astype(F32)], axis=0)
        lead = jnp.broadcast_to(lead[None], (b, CHUNK, d))
        parts.append(jnp.concatenate([lead, x.astype(F32)], axis=1).reshape(-1, d))
    h = jnp.concatenate(parts, axis=0)
    t = h.shape[0]

    first = np.zeros((t // CHUNK,), np.int32)
    last = np.zeros((t // CHUNK,), np.int32)
    pos = np.zeros((t,), np.int32)
    start = 0
    for b, seq_len in groups:
        for _ in range(b):
            first[start // CHUNK] = 1
            last[(start + seq_len) // CHUNK - 1] = 1
            pos[start:start + seq_len] = np.maximum(np.arange(seq_len) - PAD, 0)
            start += seq_len
    first, last, pos = jnp.asarray(first), jnp.asarray(last), jnp.asarray(pos)

    w_up16 = ffn_w_up.astype(BF16)
    w_dn16 = ffn_w_down.astype(BF16)
    for i in range(depth):
        kind, j = i % N_MIXERS, i // N_MIXERS
        h = _ffn(h, norm_w[i, 0], w_up16[i, 0], w_dn16[i, 0], final_norm, False)
        if kind == 0:
            h = _gated_deltanet(h, norm_w[i, 1], a_w_in[j].astype(BF16), a_conv_w[j], a_log[j],
                                a_dt_bias[j], a_o_norm[j], a_w_out[j].astype(BF16), first, last)
        elif kind == 1:
            h = _diff_attention(h, norm_w[i, 1], b_w_in[j].astype(BF16), b_lambda[j], b_sub_norm[j],
                                b_w_out[j].astype(BF16), i, groups, pos)
        else:
            h = _hgrn2(h, norm_w[i, 1], c_w_in[j].astype(BF16), c_lb_logits, i, c_o_norm[j],
                       c_w_out[j].astype(BF16), first, last)
        h = _ffn(h, norm_w[i, 2], w_up16[i, 1], w_dn16[i, 1], final_norm, i == depth - 1)

    outs = []
    start = 0
    for (b, seq_len), x in zip(groups, xs):
        rows = b * seq_len
        outs.append(h[start:start + rows].reshape(b, seq_len, d)[:, CHUNK:].astype(x.dtype))
        start += rows
    return tuple(outs)
```

```python
import functools
import math

import numpy as np
import jax
import jax.numpy as jnp
from jax import lax
from jax.experimental import pallas as pl
from jax.experimental.pallas import tpu as pltpu

F32 = jnp.float32
BF16 = jnp.bfloat16

EPS = 1e-6
CHUNK = 64
N_META = 16
PAD = CHUNK - N_META
N_MIXERS = 3
CONV_K = 5
ROPE_THETA = 500000.0
HEADS = 8
HEAD_DIM = 128
ATT_HD = 64
ATT_ROT = ATT_HD // 4
LANES = 128
V7X_VMEM_LIMIT_BYTES = 56 * 1024 * 1024

ROW_TILE = 512
CONV_TILE = 448
ATT_TQ = 256
ATT_TK = 512
LOG2E = 1.4426950408889634


def _params(*sem):
    return pltpu.CompilerParams(dimension_semantics=sem, vmem_limit_bytes=V7X_VMEM_LIMIT_BYTES)


def _dot(a, b):
    return jnp.dot(a, b, preferred_element_type=F32)


def _dot_nt(a, b):
    return lax.dot_general(a, b, (((1,), (1,)), ((), ())), preferred_element_type=F32)


def _dot_tn(a, b):
    return lax.dot_general(a, b, (((0,), (0,)), ((), ())), preferred_element_type=F32)


def _split3(x):
    hi = x.astype(BF16)
    r = x - hi.astype(F32)
    mid = r.astype(BF16)
    lo = (r - mid.astype(F32)).astype(BF16)
    return hi, mid, lo


def _mask_dot(mask3, x):
    return _dot(mask3, jnp.concatenate(_split3(x), axis=0))


def _dot_mask_nt(x, mask):
    hi, mid, lo = _split3(x)
    return _dot_nt(hi, mask) + _dot_nt(mid, mask) + _dot_nt(lo, mask)


def _order_mask3(d):
    ii = lax.broadcasted_iota(jnp.int32, (CHUNK, 3 * CHUNK), 0)
    ss = lax.broadcasted_iota(jnp.int32, (CHUNK, 3 * CHUNK), 1) & (CHUNK - 1)
    return jnp.where(ii >= ss if d == 0 else ii <= ss, 1.0, 0.0).astype(BF16)


def _rms(x, w):
    return x * lax.rsqrt(jnp.mean(x * x, axis=-1, keepdims=True) + EPS) * w


def _silu(x):
    return x * jax.nn.sigmoid(x)


def _softplus(x):
    return jnp.maximum(x, 0.0) + jnp.log(1.0 + jnp.exp(-jnp.abs(x)))


def _resident(shape):
    nd = len(shape)
    return pl.BlockSpec(shape, lambda *_: (0,) * nd, pipeline_mode=pl.Buffered(1))


def _ffn_kernel(x_ref, nw_ref, wup_ref, wdn_ref, fn_ref, o_ref, *, d_ff, n_split, final):
    x = x_ref[...]
    xn = _rms(x, nw_ref[...]).astype(BF16)
    step = d_ff // n_split
    y = None
    for c in range(n_split):
        g = _dot(xn, wup_ref[:, c * step:(c + 1) * step])
        u = _dot(xn, wup_ref[:, d_ff + c * step:d_ff + (c + 1) * step])
        a = (_silu(g) * u).astype(BF16)
        part = _dot(a, wdn_ref[c * step:(c + 1) * step, :])
        y = part if y is None else y + part
    h = x + 0.5 * y
    if final:
        h = _rms(h, fn_ref[...])
    o_ref[...] = h


def _ffn(h, nw, w_up, w_down, final_w, final):
    t, d = h.shape
    d_ff = w_down.shape[0]
    n_split = 2 if (d_ff // 2) % LANES == 0 else 1
    row = pl.BlockSpec((ROW_TILE, d), lambda i: (i, 0))
    return pl.pallas_call(
        functools.partial(_ffn_kernel, d_ff=d_ff, n_split=n_split, final=final),
        grid=(pl.cdiv(t, ROW_TILE),),
        in_specs=[row, _resident((1, d)), _resident(w_up.shape), _resident(w_down.shape),
                  _resident((1, d))],
        out_specs=row,
        out_shape=jax.ShapeDtypeStruct((t, d), F32),
        compiler_params=_params("parallel"),
        name="ffn",
    )(h, nw.reshape(1, d), w_up, w_down, final_w.reshape(1, d))


def _proj_kernel(x_ref, nw_ref, *refs):
    n = len(refs) // 2
    xn = _rms(x_ref[...], nw_ref[...]).astype(BF16)
    for w_ref, o_ref in zip(refs[:n], refs[n:]):
        o_ref[...] = _dot(xn, w_ref[...])


def _norm_proj(h, nw, weights):
    t, d = h.shape
    row = pl.BlockSpec((ROW_TILE, d), lambda i: (i, 0))
    return pl.pallas_call(
        _proj_kernel,
        grid=(pl.cdiv(t, ROW_TILE),),
        in_specs=[row, _resident((1, d))] + [_resident(w.shape) for w in weights],
        out_specs=[pl.BlockSpec((ROW_TILE, w.shape[1]), lambda i: (i, 0)) for w in weights],
        out_shape=[jax.ShapeDtypeStruct((t, w.shape[1]), F32) for w in weights],
        compiler_params=_params("parallel"),
        name="norm_proj",
    )(h, nw.reshape(1, d), *weights)


def _out_kernel(h_ref, *refs, n_o, gated, scale):
    o_refs = refs[:n_o]
    rest = refs[n_o:]
    if gated:
        gate_ref, on_ref, w_ref, out_ref = rest
    else:
        on_ref, w_ref, out_ref = rest
    o = o_refs[0][...]
    for r in o_refs[1:]:
        o = o + r[...]
    on = on_ref[...]
    parts = []
    for hd in range(HEADS):
        sl = slice(hd * HEAD_DIM, (hd + 1) * HEAD_DIM)
        y = _rms(o[:, sl], on)
        if gated:
            y = y * _silu(gate_ref[:, sl])
        else:
            y = y * scale
        parts.append(y.astype(BF16))
    y = jnp.concatenate(parts, axis=-1)
    out_ref[...] = h_ref[...] + _dot(y, w_ref[...])


def _out_proj(h, o_list, gate, gate_block, o_norm, w_out, scale=1.0):
    t, d = h.shape
    row = pl.BlockSpec((ROW_TILE, d), lambda i: (i, 0))
    in_specs = [row]
    args = [h]
    for arr, lead in o_list:
        if lead is None:
            in_specs.append(row)
        else:
            in_specs.append(pl.BlockSpec((None, ROW_TILE, d), lambda i, lead=lead: (lead, i, 0)))
        args.append(arr)
    gated = gate is not None
    if gated:
        in_specs.append(pl.BlockSpec((ROW_TILE, d), lambda i: (i, gate_block)))
        args.append(gate)
    in_specs += [_resident((1, HEAD_DIM)), _resident(w_out.shape)]
    args += [o_norm.reshape(1, HEAD_DIM), w_out]
    return pl.pallas_call(
        functools.partial(_out_kernel, n_o=len(o_list), gated=gated, scale=scale),
        grid=(pl.cdiv(t, ROW_TILE),),
        in_specs=in_specs,
        out_specs=row,
        out_shape=jax.ShapeDtypeStruct((t, d), F32),
        compiler_params=_params("parallel"),
        name="out_proj",
    )(*args)


def _conv_kernel(first_ref, xm_ref, xp_ref, xn_ref, w_ref, o_ref, buf_ref, *, tile):
    i = pl.program_id(0)
    j = pl.program_id(1)
    last = pl.num_programs(0) - 1
    buf_ref[8:8 + tile, :] = xm_ref[...]
    buf_ref[0:8, :] = jnp.where(i > 0, xp_ref[...], 0.0)
    buf_ref[8 + tile:16 + tile, :] = jnp.where(i < last, xn_ref[...], 0.0)
    half = (CONV_K - 1) // 2
    acc = None
    for k in range(CONV_K):
        term = w_ref[k:k + 1, :] * buf_ref[8 - half + k:8 - half + k + tile, :]
        acc = term if acc is None else acc + term
    norm_scale = jnp.where(j == 0, HEAD_DIM ** -0.5, 1.0)
    row = lax.broadcasted_iota(jnp.int32, (CHUNK, 1), 0)
    per_tile = tile // CHUNK
    for c in range(per_tile):
        n_dead = jnp.where(first_ref[i * per_tile + c] > 0, PAD, 0)
        y = jnp.where(row < n_dead, 0.0, acc[c * CHUNK:(c + 1) * CHUNK, :])
        y = _silu(y)
        for hd in range(HEADS):
            sl = slice(hd * HEAD_DIM, (hd + 1) * HEAD_DIM)
            blk = y[:, sl]
            inv = lax.rsqrt(jnp.sum(blk * blk, axis=-1, keepdims=True) + EPS) * norm_scale
            inv = jnp.where(j < 2, inv, 1.0)
            o_ref[c * CHUNK:(c + 1) * CHUNK, sl] = blk * inv


def _conv_qkv(proj, conv_w, first):
    t = proj.shape[0]
    width = HEADS * HEAD_DIM
    tile = CONV_TILE if t % CONV_TILE == 0 else CHUNK
    nb8 = t // 8
    tb8 = tile // 8
    grid_spec = pltpu.PrefetchScalarGridSpec(
        num_scalar_prefetch=1,
        grid=(t // tile, 3),
        in_specs=[
            pl.BlockSpec((tile, width), lambda i, j, f: (i, j)),
            pl.BlockSpec((8, width), lambda i, j, f: (jnp.maximum(i * tb8 - 1, 0), j)),
            pl.BlockSpec((8, width), lambda i, j, f: (jnp.minimum((i + 1) * tb8, nb8 - 1), j)),
            pl.BlockSpec((CONV_K, width), lambda i, j, f: (0, j)),
        ],
        out_specs=pl.BlockSpec((tile, width), lambda i, j, f: (i, j)),
        scratch_shapes=[pltpu.VMEM((tile + 16, width), F32)],
    )
    return pl.pallas_call(
        functools.partial(_conv_kernel, tile=tile),
        grid_spec=grid_spec,
        out_shape=jax.ShapeDtypeStruct((t, 3 * width), F32),
        compiler_params=_params("arbitrary", "arbitrary"),
        name="conv_qkv",
    )(first, proj, proj, proj, conv_w)


def _scan_chunk(d, s, nc):
    return s if d == 0 else nc - 1 - s


def _order_masks(d):
    ii = lax.broadcasted_iota(jnp.int32, (CHUNK, CHUNK), 0)
    jj = lax.broadcasted_iota(jnp.int32, (CHUNK, CHUNK), 1)
    return (ii >= jj, ii > jj) if d == 0 else (ii <= jj, ii < jj)


def _delta_kernel(first_ref, last_ref, qf_ref, kf_ref, vf_ref, qb_ref, kb_ref, vb_ref,
                  bacf_ref, bacb_ref, barf_ref, barb_ref,
                  alog_r_ref, dtb_r_ref, alog_c_ref, dtb_c_ref, of_ref, ob_ref, s_ref):
    s = pl.program_id(0)
    nc = pl.num_programs(0)

    @pl.when(first_ref[s] > 0)
    def _():
        s_ref[0] = jnp.zeros_like(s_ref[0])

    @pl.when(last_ref[nc - 1 - s] > 0)
    def _():
        s_ref[1] = jnp.zeros_like(s_ref[1])

    ii = lax.broadcasted_iota(jnp.int32, (CHUNK, CHUNK), 0)
    jj = lax.broadcasted_iota(jnp.int32, (CHUNK, CHUNK), 1)
    eye = jnp.where(ii == jj, 1.0, 0.0)
    n_lvl = int(math.log2(CHUNK))
    lvl = [((ii >> (b + 1)) == (jj >> (b + 1))) & ((ii >> b) != (jj >> b)) for b in range(n_lvl)]

    q_refs, k_refs, v_refs = (qf_ref, qb_ref), (kf_ref, kb_ref), (vf_ref, vb_ref)
    o_refs = (of_ref, ob_ref)
    incl, strict, beta_c, gc_c, gc_r, gt_c = [], [], [], [], [], []
    for d, (bac_ref, bar_ref) in enumerate(((bacf_ref, barf_ref), (bacb_ref, barb_ref))):
        inc, strc = _order_masks(d)
        m_incl = _order_mask3(d)
        bac = bac_ref[...]
        bar = bar_ref[...]
        la_c = -jnp.exp(alog_r_ref[d]) * _softplus(bac[:, HEADS:] + dtb_r_ref[d])
        la_r = -jnp.exp(alog_c_ref[d]) * _softplus(bar[HEADS:, :] + dtb_c_ref[d])
        incl.append(inc)
        strict.append(strc)
        beta_c.append(jax.nn.sigmoid(bac[:, :HEADS]))
        gc_c.append(_mask_dot(m_incl, la_c))
        gc_r.append(_dot_mask_nt(la_r, jnp.where(inc, 1.0, 0.0).astype(BF16)))
        gt_c.append(jnp.sum(la_c, axis=0, keepdims=True))

    units = [(d, hd) for d in range(2) for hd in range(HEADS)]
    n = range(len(units))
    sls = [slice(hd * HEAD_DIM, (hd + 1) * HEAD_DIM) for _, hd in units]
    qs = [q_refs[d][:, sls[x]] for x, (d, _) in enumerate(units)]
    ks = [k_refs[d][:, sls[x]] for x, (d, _) in enumerate(units)]
    vs = [v_refs[d][:, sls[x]] for x, (d, _) in enumerate(units)]
    g_col = [gc_c[d][:, hd:hd + 1] for d, hd in units]
    g_tot = [gt_c[d][:, hd:hd + 1] for d, hd in units]
    b_col = [beta_c[d][:, hd:hd + 1] for d, hd in units]
    decay = [jnp.exp(jnp.where(incl[d], g_col[x] - gc_r[d][hd:hd + 1, :], -jnp.inf))
             for x, (d, hd) in enumerate(units)]
    kb = [ks[x] * b_col[x] for x in n]
    k16 = [x.astype(BF16) for x in ks]
    a_mat = [jnp.where(strict[units[x][0]], _dot_nt(kb[x].astype(BF16), k16[x]) * decay[x], 0.0) for x in n]
    t_inv = [eye - jnp.where(lvl[0], a, 0.0) for a in a_mat]
    for b in range(1, n_lvl):
        t16 = [t.astype(BF16) for t in t_inv]
        a_off = [jnp.where(lvl[b], a, 0.0).astype(BF16) for a in a_mat]
        x16 = [_dot(a_off[x], t16[x]).astype(BF16) for x in n]
        t_inv = [t_inv[x] - _dot(t16[x], x16[x]) for x in n]
    t16 = [t.astype(BF16) for t in t_inv]
    e_col = [jnp.exp(g) for g in g_col]
    u = [_dot(t16[x], (vs[x] * b_col[x]).astype(BF16)) for x in n]
    w16 = [_dot(t16[x], (kb[x] * e_col[x]).astype(BF16)).astype(BF16) for x in n]
    qk16 = [(_dot_nt(qs[x].astype(BF16), k16[x]) * decay[x]).astype(BF16) for x in n]
    qd16 = [(qs[x] * e_col[x]).astype(BF16) for x in n]
    kd16 = [(ks[x] * jnp.exp(g_tot[x] - g_col[x])).astype(BF16) for x in n]
    state = [s_ref[d, hd] for d, hd in units]
    st16 = [x.astype(BF16) for x in state]
    vn16 = [(u[x] - _dot(w16[x], st16[x])).astype(BF16) for x in n]
    for x, (d, _) in enumerate(units):
        o_refs[d][:, sls[x]] = _dot(qd16[x], st16[x]) + _dot(qk16[x], vn16[x])
    for x, (d, hd) in enumerate(units):
        s_ref[d, hd] = state[x] * jnp.exp(g_tot[x]) + _dot_tn(kd16[x], vn16[x])


def _delta_rule(qkv, ba_c, ba_r, a_log, dt_bias, first, last):
    t = qkv.shape[0]
    nc = t // CHUNK
    width = HEADS * HEAD_DIM

    def col(j, d):
        return pl.BlockSpec((CHUNK, width), lambda s, f, l: (_scan_chunk(d, s, nc), j))

    def bac(d):
        return pl.BlockSpec((None, CHUNK, 2 * HEADS), lambda s, f, l: (d, _scan_chunk(d, s, nc), 0))

    def bar(d):
        return pl.BlockSpec((None, None, 2 * HEADS, CHUNK), lambda s, f, l: (d, _scan_chunk(d, s, nc), 0, 0))

    par_r = pl.BlockSpec((2, 1, HEADS), lambda s, f, l: (0, 0, 0))
    par_c = pl.BlockSpec((2, HEADS, 1), lambda s, f, l: (0, 0, 0))
    grid_spec = pltpu.PrefetchScalarGridSpec(
        num_scalar_prefetch=2,
        grid=(nc,),
        in_specs=[col(0, 0), col(1, 0), col(2, 0), col(0, 1), col(1, 1), col(2, 1),
                  bac(0), bac(1), bar(0), bar(1), par_r, par_r, par_c, par_c],
        out_specs=[pl.BlockSpec((CHUNK, width), lambda s, f, l, d=d: (_scan_chunk(d, s, nc), 0))
                   for d in range(2)],
        scratch_shapes=[pltpu.VMEM((2, HEADS, HEAD_DIM, HEAD_DIM), F32)],
    )
    return pl.pallas_call(
        _delta_kernel,
        grid_spec=grid_spec,
        out_shape=[jax.ShapeDtypeStruct((t, width), F32)] * 2,
        compiler_params=_params("arbitrary"),
        name="delta_rule",
    )(first, last, qkv, qkv, qkv, qkv, qkv, qkv, ba_c, ba_c, ba_r, ba_r,
      a_log.reshape(2, 1, HEADS), dt_bias.reshape(2, 1, HEADS),
      a_log.reshape(2, HEADS, 1), dt_bias.reshape(2, HEADS, 1))


def _gated_deltanet(h, nw, w_in, conv_w, a_log, dt_bias, o_norm, w_out, first, last):
    t = h.shape[0]
    width = HEADS * HEAD_DIM
    proj, ba = _norm_proj(h, nw, [w_in[:, :4 * width], w_in[:, 4 * width:]])
    qkv = _conv_qkv(proj, conv_w, first)
    ba = ba.reshape(t, 2, 2, HEADS)
    ba_c = jnp.transpose(ba, (2, 0, 1, 3)).reshape(2, t, 2 * HEADS)
    ba_r = jnp.transpose(ba_c.reshape(2, t // CHUNK, CHUNK, 2 * HEADS), (0, 1, 3, 2))
    o_f, o_b = _delta_rule(qkv, ba_c, ba_r, a_log, dt_bias, first, last)
    return _out_proj(h, [(o_f, None), (o_b, None)], proj, 3, o_norm, w_out)


def _gla_level_matrices():
    ii = np.arange(CHUNK)[:, None]
    jj = np.arange(CHUNK)[None, :]
    out = []
    for d in range(2):
        mats = [(ii >= jj) if d == 0 else (ii <= jj)]
        for b in range(int(math.log2(CHUNK))):
            size = 1 << b
            mid = ((ii >> (b + 1)) << (b + 1)) + size
            upper = (ii & size) != 0
            if d == 0:
                sel = (upper & (jj >= mid) & (jj <= ii)) | (~upper & (jj > ii) & (jj < mid))
            else:
                sel = (upper & (jj >= mid) & (jj < ii)) | (~upper & (jj >= ii) & (jj < mid))
            mats.append(sel)
        out.append(np.tile(np.concatenate(mats, axis=0), (1, 3)))
    return np.stack(out).astype(np.float32)


def _gla_kernel(first_ref, last_ref, qf_ref, vf_ref, ff_ref, qb_ref, vb_ref, fb_ref, lb_ref, lvl_ref,
                of_ref, ob_ref, s_ref, *, layer_idx):
    s = pl.program_id(0)
    nc = pl.num_programs(0)

    @pl.when(first_ref[s] > 0)
    def _():
        s_ref[0] = jnp.zeros_like(s_ref[0])

    @pl.when(last_ref[nc - 1 - s] > 0)
    def _():
        s_ref[1] = jnp.zeros_like(s_ref[1])

    ii = lax.broadcasted_iota(jnp.int32, (CHUNK, CHUNK), 0)
    jj = lax.broadcasted_iota(jnp.int32, (CHUNK, CHUNK), 1)
    n_lvl = int(math.log2(CHUNK))

    lbl = lb_ref[...]
    lbw = jnp.exp(lbl - jnp.max(lbl, axis=0, keepdims=True))
    lbw = lbw / jnp.sum(lbw, axis=0, keepdims=True)
    lb = jnp.zeros_like(lbw[0:1, :])
    for r in range(1, layer_idx + 1):
        lb = lb + lbw[r:r + 1, :]

    q_refs, v_refs, f_refs, o_refs = (qf_ref, qb_ref), (vf_ref, vb_ref), (ff_ref, fb_ref), (of_ref, ob_ref)
    pair_masks, q, kk, x_all, q_dec, k_dec, f_last = [], [], [], [], [], [], []
    for d in range(2):
        _, strict = _order_masks(d)
        pair_masks.append([((ii >> (b + 1)) == (jj >> (b + 1))) & ((ii >> b) != (jj >> b)) & strict
                           for b in range(n_lvl)])
        f = lb + (1.0 - lb) * jax.nn.sigmoid(f_refs[d][...])
        logf = jnp.log(f)
        e_all = _mask_dot(lvl_ref[d], logf)
        tot = jnp.sum(logf, axis=0, keepdims=True)
        xa = jnp.exp(e_all)
        qd = _silu(q_refs[d][...]) * (HEAD_DIM ** -0.5)
        q.append(qd)
        kk.append(1.0 - f)
        x_all.append(xa)
        q_dec.append(qd * xa[:CHUNK, :])
        k_dec.append((1.0 - f) * jnp.exp(tot - e_all[:CHUNK, :]))
        f_last.append(jnp.exp(tot))

    units = [(d, hd) for d in range(2) for hd in range(HEADS)]
    n = range(len(units))
    sls = [slice(hd * HEAD_DIM, (hd + 1) * HEAD_DIM) for _, hd in units]
    qs = [q[d][:, sls[x]] for x, (d, _) in enumerate(units)]
    ks = [kk[d][:, sls[x]] for x, (d, _) in enumerate(units)]
    attn = [jnp.where(ii == jj, _dot_nt(qs[x].astype(BF16), ks[x].astype(BF16)), 0.0) for x in n]
    for b in range(n_lvl):
        xb = [x_all[d][(b + 1) * CHUNK:(b + 2) * CHUNK, sls[x]] for x, (d, _) in enumerate(units)]
        part = [_dot_nt((qs[x] * xb[x]).astype(BF16), (ks[x] * xb[x]).astype(BF16)) for x in n]
        attn = [jnp.where(pair_masks[units[x][0]][b], part[x], attn[x]) for x in n]
    v16 = [v_refs[d][:, sls[x]].astype(BF16) for x, (d, _) in enumerate(units)]
    st_t = [s_ref[d, hd] for d, hd in units]
    for x, (d, _) in enumerate(units):
        o_refs[d][:, sls[x]] = (_dot_nt(q_dec[d][:, sls[x]].astype(BF16), st_t[x].astype(BF16))
                                + _dot(attn[x].astype(BF16), v16[x]))
    for x, (d, hd) in enumerate(units):
        s_ref[d, hd] = st_t[x] * f_last[d][:, sls[x]] + _dot_tn(v16[x], k_dec[d][:, sls[x]].astype(BF16))


def _gla(proj, lb_logits, layer_idx, first, last):
    t = proj.shape[0]
    nc = t // CHUNK
    width = HEADS * HEAD_DIM

    lvl = jnp.asarray(_gla_level_matrices(), BF16)

    def col(j, d):
        return pl.BlockSpec((CHUNK, width), lambda s, f, l: (_scan_chunk(d, s, nc), j))

    grid_spec = pltpu.PrefetchScalarGridSpec(
        num_scalar_prefetch=2,
        grid=(nc,),
        in_specs=[col(0, 0), col(1, 0), col(3, 0), col(0, 1), col(1, 1), col(4, 1),
                  pl.BlockSpec(lb_logits.shape, lambda s, f, l: (0, 0)),
                  pl.BlockSpec(lvl.shape, lambda s, f, l: (0, 0, 0))],
        out_specs=[col(0, 0), col(0, 1)],
        scratch_shapes=[pltpu.VMEM((2, HEADS, HEAD_DIM, HEAD_DIM), F32)],
    )
    return pl.pallas_call(
        functools.partial(_gla_kernel, layer_idx=layer_idx),
        grid_spec=grid_spec,
        out_shape=[jax.ShapeDtypeStruct((t, width), F32)] * 2,
        compiler_params=_params("arbitrary"),
        name="gla",
    )(first, last, proj, proj, proj, proj, proj, proj, lb_logits, lvl)


def _hgrn2(h, nw, w_in, lb_logits, layer_idx, o_norm, w_out, first, last):
    (proj,) = _norm_proj(h, nw, [w_in])
    o_f, o_b = _gla(proj, lb_logits, layer_idx, first, last)
    return _out_proj(h, [(o_f, None), (o_b, None)], proj, 2, o_norm, w_out)


def _rope_kernel(x_ref, cos_ref, s1_ref, s2_ref, o_ref, *, n_rot_tiles, n_q_tiles):
    cos = cos_ref[...]
    s1 = s1_ref[...]
    s2 = s2_ref[...]
    for c in range(n_rot_tiles):
        sl = slice(c * LANES, (c + 1) * LANES)
        x = x_ref[:, sl]
        half = ATT_ROT // 2
        y = x * cos + pltpu.roll(x, LANES - half, 1) * s1 + pltpu.roll(x, half, 1) * s2
        if c < n_q_tiles:
            y = y * (ATT_HD ** -0.5 * LOG2E)
        o_ref[:, sl] = y.astype(BF16)
    o_ref[:, n_rot_tiles * LANES:] = x_ref[:, n_rot_tiles * LANES:].astype(BF16)


def _rope(proj, cos_t, s1_t, s2_t):
    t, n = proj.shape
    qk_w = HEADS * ATT_HD
    row = pl.BlockSpec((ATT_TQ, n), lambda i: (i, 0))
    tab = pl.BlockSpec((ATT_TQ, LANES), lambda i: (i, 0))
    return pl.pallas_call(
        functools.partial(_rope_kernel, n_rot_tiles=4 * qk_w // LANES, n_q_tiles=2 * qk_w // LANES),
        grid=(pl.cdiv(t, ATT_TQ),),
        in_specs=[row, tab, tab, tab],
        out_specs=row,
        out_shape=jax.ShapeDtypeStruct((t, n), BF16),
        compiler_params=_params("parallel"),
        name="rope",
    )(proj, cos_t, s1_t, s2_t)


def _attn_kernel(q1_ref, q2_ref, k1_ref, k2_ref, v_ref, lam_ref, o_ref, vt_ref, *scratch,
                 seq_len, lam_init):
    i = pl.program_id(2)
    n_full = seq_len // ATT_TK
    tail = seq_len - n_full * ATT_TK
    acc_refs, sc_refs = scratch[:4], scratch[4:]

    @pl.when(i == 0)
    def _():
        def body(j, carry):
            blk = v_ref[pl.ds(pl.multiple_of(j * ATT_TK, ATT_TK), ATT_TK), :]
            vt_ref[j] = blk.astype(F32).T.astype(BF16)
            return carry
        if n_full > 0:
            lax.fori_loop(0, n_full, body, 0)
        if tail > 0:
            blk = v_ref[n_full * ATT_TK:seq_len, :]
            vt_ref[n_full, :, :tail] = blk.astype(F32).T.astype(BF16)

    lane = lax.broadcasted_iota(jnp.int32, (1, LANES), 1)
    q_refs = (q1_ref, q2_ref)
    k_refs = (k1_ref, k2_ref)
    qm = []
    for hh in range(2):
        keep = (lane >= hh * ATT_HD) & (lane < (hh + 1) * ATT_HD)
        qm.append([jnp.where(keep, q_refs[m][...], jnp.zeros((), BF16)) for m in range(2)])

    for a in acc_refs:
        a[...] = jnp.zeros_like(a)
    units = [(hh, m) for hh in range(2) for m in range(2)]

    def scores(j, size, masked):
        start = j * ATT_TK
        if not isinstance(j, int):
            start = pl.multiple_of(start, ATT_TK)
        k_t = [k_refs[m][pl.ds(start, size), :] for m in range(2)]
        sc = [_dot_nt(k_t[m], qm[hh][m]) for hh, m in units]
        if masked:
            rowid = lax.broadcasted_iota(jnp.int32, (size, 1), 0)
            sc = [jnp.where(rowid >= PAD, s, -jnp.inf) for s in sc]
        return sc

    def consume(sc, j, size, carry):
        m_old, l_old = carry[:4], carry[4:]
        if size == ATT_TK:
            vt = [vt_ref[j, hh * HEAD_DIM:(hh + 1) * HEAD_DIM, :] for hh in range(2)]
        else:
            vt = [vt_ref[j, hh * HEAD_DIM:(hh + 1) * HEAD_DIM, :size] for hh in range(2)]
        m_new = [jnp.maximum(m_old[u], jnp.max(sc[u], axis=0, keepdims=True)) for u in range(4)]
        alpha = [jnp.exp2(m_old[u] - m_new[u]) for u in range(4)]
        p = [jnp.exp2(sc[u] - m_new[u]) for u in range(4)]
        l_new = [alpha[u] * l_old[u] + jnp.sum(p[u], axis=0, keepdims=True) for u in range(4)]
        pv = [_dot(vt[units[u][0]], p[u].astype(BF16)) for u in range(4)]
        for u in range(4):
            acc_refs[u][...] = alpha[u] * acc_refs[u][...] + pv[u]
        return tuple(m_new) + tuple(l_new)

    def put(buf, sc):
        for u in range(4):
            buf[u][...] = sc[u]

    def get(buf):
        return [buf[u][...] for u in range(4)]

    carry = (jnp.full((1, ATT_TQ), -jnp.inf, F32),) * 4 + (jnp.zeros((1, ATT_TQ), F32),) * 4
    bufs = (sc_refs[:4], sc_refs[4:])
    if n_full > 0:
        put(bufs[0], scores(0, ATT_TK, True))
        n_pairs = max(0, (n_full - 2) // 2)
        if n_pairs > 0:
            def body(i, c):
                put(bufs[1], scores(2 * i + 1, ATT_TK, False))
                c = consume(get(bufs[0]), 2 * i, ATT_TK, c)
                put(bufs[0], scores(2 * i + 2, ATT_TK, False))
                return consume(get(bufs[1]), 2 * i + 1, ATT_TK, c)
            carry = lax.fori_loop(0, n_pairs, body, carry)
        tail_sc = None
        for t in range(2 * n_pairs, n_full):
            if t + 1 < n_full:
                put(bufs[(t + 1) % 2], scores(t + 1, ATT_TK, False))
            elif tail > 0:
                tail_sc = scores(n_full, tail, False)
            carry = consume(get(bufs[t % 2]), t, ATT_TK, carry)
        if tail > 0:
            carry = consume(tail_sc, n_full, tail, carry)
    else:
        carry = consume(scores(0, tail, True), 0, tail, carry)
    l_fin = carry[4:]

    lv = lam_ref[...]
    lam = (jnp.exp(jnp.sum(lv[0:1] * lv[1:2], axis=-1, keepdims=True))
           - jnp.exp(jnp.sum(lv[2:3] * lv[3:4], axis=-1, keepdims=True)) + lam_init)
    row = i * ATT_TQ + lax.broadcasted_iota(jnp.int32, (ATT_TQ, 1), 0)
    for hh in range(2):
        o_t = (acc_refs[2 * hh][...] / l_fin[2 * hh]
               - lam * (acc_refs[2 * hh + 1][...] / l_fin[2 * hh + 1]))
        o_ref[:, hh * HEAD_DIM:(hh + 1) * HEAD_DIM] = jnp.where(row >= PAD, o_t.T, 0.0)


def _attention(qkv, lam_vec, lam_init):
    b, seq_len, _ = qkv.shape
    width = HEADS * HEAD_DIM
    n_pair = HEADS // 2
    pair_w = 2 * ATT_HD
    nq = pl.cdiv(seq_len, ATT_TQ)

    def qspec(off):
        return pl.BlockSpec((None, ATT_TQ, pair_w), lambda bb, p, i: (bb, i, off + p))

    def kspec(off):
        return pl.BlockSpec((None, seq_len, pair_w), lambda bb, p, i: (bb, 0, off + p))

    return pl.pallas_call(
        functools.partial(_attn_kernel, seq_len=seq_len, lam_init=lam_init),
        grid=(b, n_pair, nq),
        in_specs=[
            qspec(0), qspec(n_pair), kspec(2 * n_pair), kspec(3 * n_pair),
            pl.BlockSpec((None, seq_len, 2 * HEAD_DIM), lambda bb, p, i: (bb, 0, 2 * n_pair + p)),
            pl.BlockSpec(lam_vec.shape, lambda bb, p, i: (0, 0)),
        ],
        out_specs=pl.BlockSpec((None, ATT_TQ, 2 * HEAD_DIM), lambda bb, p, i: (bb, i, p)),
        out_shape=jax.ShapeDtypeStruct((b, seq_len, width), F32),
        scratch_shapes=[pltpu.VMEM((pl.cdiv(seq_len, ATT_TK), 2 * HEAD_DIM, ATT_TK), BF16)]
        + [pltpu.VMEM((HEAD_DIM, ATT_TQ), F32)] * 4
        + [pltpu.VMEM((ATT_TK, ATT_TQ), F32)] * 8,
        compiler_params=_params("arbitrary", "arbitrary", "arbitrary"),
        name="diff_attention",
    )(qkv, qkv, qkv, qkv, qkv, lam_vec)


def _rope_tables(pos):
    half = ATT_ROT // 2
    inv_freq = jnp.exp(-math.log(ROPE_THETA) * jnp.arange(half, dtype=F32) / half)
    ang = pos.astype(F32)[:, None] * inv_freq
    cos, sin = jnp.cos(ang), jnp.sin(ang)
    n = pos.shape[0]
    rest = ATT_HD - ATT_ROT
    cos_t = jnp.concatenate([cos, cos, jnp.ones((n, rest), F32)], axis=1)
    s1_t = jnp.concatenate([-sin, jnp.zeros((n, ATT_HD - half), F32)], axis=1)
    s2_t = jnp.concatenate([jnp.zeros((n, half), F32), sin, jnp.zeros((n, rest), F32)], axis=1)
    rep = LANES // ATT_HD
    return tuple(jnp.tile(x, (1, rep)) for x in (cos_t, s1_t, s2_t))


def _diff_attention(h, nw, w_in, lam_vec, sub_norm, w_out, layer_idx, groups, pos):
    (proj,) = _norm_proj(h, nw, [w_in])
    roped = _rope(proj, *_rope_tables(pos))
    lam_init = 0.8 - 0.6 * math.exp(-0.3 * layer_idx)
    outs = []
    start = 0
    for b, seq_len in groups:
        rows = b * seq_len
        part = roped[start:start + rows].reshape(b, seq_len, roped.shape[1])
        outs.append(_attention(part, lam_vec, lam_init).reshape(rows, -1))
        start += rows
    o = jnp.concatenate(outs, axis=0)
    return _out_proj(h, [(o, None)], None, 0, sub_norm, w_out, scale=1.0 - lam_init)


def kernel(x_prompt, x_sample, meta_tokens, norm_w, ffn_w_up, ffn_w_down,
           a_w_in, a_conv_w, a_log, a_dt_bias, a_o_norm, a_w_out,
           b_w_in, b_lambda, b_sub_norm, b_w_out,
           c_w_in, c_lb_logits, c_o_norm, c_w_out, final_norm):
    d = x_prompt.shape[-1]
    depth = norm_w.shape[0]
    xs = (x_prompt, x_sample)
    groups = [(x.shape[0], x.shape[1] + CHUNK) for x in xs]

    parts = []
    for x in xs:
        b = x.shape[0]
        lead = jnp.concatenate([jnp.zeros((PAD, d), F32), meta_tokens.astype(F32)], axis=0)
        lead = jnp.broadcast_to(lead[None], (b, CHUNK, d))
        parts.append(jnp.concatenate([lead, x.astype(F32)], axis=1).reshape(-1, d))
    h = jnp.concatenate(parts, axis=0)
    t = h.shape[0]

    first = np.zeros((t // CHUNK,), np.int32)
    last = np.zeros((t // CHUNK,), np.int32)
    pos = np.zeros((t,), np.int32)
    start = 0
    for b, seq_len in groups:
        for _ in range(b):
            first[start // CHUNK] = 1
            last[(start + seq_len) // CHUNK - 1] = 1
            pos[start:start + seq_len] = np.maximum(np.arange(seq_len) - PAD, 0)
            start += seq_len
    first, last, pos = jnp.asarray(first), jnp.asarray(last), jnp.asarray(pos)

    w_up16 = ffn_w_up.astype(BF16)
    w_dn16 = ffn_w_down.astype(BF16)
    for i in range(depth):
        kind, j = i % N_MIXERS, i // N_MIXERS
        h = _ffn(h, norm_w[i, 0], w_up16[i, 0], w_dn16[i, 0], final_norm, False)
        if kind == 0:
            h = _gated_deltanet(h, norm_w[i, 1], a_w_in[j].astype(BF16), a_conv_w[j], a_log[j],
                                a_dt_bias[j], a_o_norm[j], a_w_out[j].astype(BF16), first, last)
        elif kind == 1:
            h = _diff_attention(h, norm_w[i, 1], b_w_in[j].astype(BF16), b_lambda[j], b_sub_norm[j],
                                b_w_out[j].astype(BF16), i, groups, pos)
        else:
            h = _hgrn2(h, norm_w[i, 1], c_w_in[j].astype(BF16), c_lb_logits, i, c_o_norm[j],
                       c_w_out[j].astype(BF16), first, last)
        h = _ffn(h, norm_w[i, 2], w_up16[i, 1], w_dn16[i, 1], final_norm, i == depth - 1)

    outs = []
    start = 0
    for (b, seq_len), x in zip(groups, xs):
        rows = b * seq_len
        outs.append(h[start:start + rows].reshape(b, seq_len, d)[:, CHUNK:].astype(x.dtype))
        start += rows
    return tuple(outs)
```

```python
import functools
import math

import numpy as np
import jax
import jax.numpy as jnp
from jax import lax
from jax.experimental import pallas as pl
from jax.experimental.pallas import tpu as pltpu

F32 = jnp.float32
BF16 = jnp.bfloat16

EPS = 1e-6
CHUNK = 64
N_META = 16
PAD = CHUNK - N_META
N_MIXERS = 3
CONV_K = 5
ROPE_THETA = 500000.0
HEADS = 8
HEAD_DIM = 128
ATT_HD = 64
ATT_ROT = ATT_HD // 4
LANES = 128
V7X_VMEM_LIMIT_BYTES = 56 * 1024 * 1024

ROW_TILE = 512
CONV_TILE = 448
ATT_TQ = 256
ATT_TK = 512
LOG2E = 1.4426950408889634
FFN_MAX_SPLIT = 11
DELTA_CHUNKS_PER_STEP = 2


def _params(*sem):
    return pltpu.CompilerParams(dimension_semantics=sem, vmem_limit_bytes=V7X_VMEM_LIMIT_BYTES)


def _dot(a, b):
    return jnp.dot(a, b, preferred_element_type=F32)


def _dot_nt(a, b):
    return lax.dot_general(a, b, (((1,), (1,)), ((), ())), preferred_element_type=F32)


def _dot_tn(a, b):
    return lax.dot_general(a, b, (((0,), (0,)), ((), ())), preferred_element_type=F32)


def _split3(x):
    hi = x.astype(BF16)
    r = x - hi.astype(F32)
    mid = r.astype(BF16)
    lo = (r - mid.astype(F32)).astype(BF16)
    return hi, mid, lo


def _mask_dot(mask3, x):
    return _dot(mask3, jnp.concatenate(_split3(x), axis=0))


def _dot_mask_nt(x, mask):
    hi, mid, lo = _split3(x)
    return _dot_nt(hi, mask) + _dot_nt(mid, mask) + _dot_nt(lo, mask)


def _order_mask3(d):
    ii = lax.broadcasted_iota(jnp.int32, (CHUNK, 3 * CHUNK), 0)
    ss = lax.broadcasted_iota(jnp.int32, (CHUNK, 3 * CHUNK), 1) & (CHUNK - 1)
    return jnp.where(ii >= ss if d == 0 else ii <= ss, 1.0, 0.0).astype(BF16)


def _rms(x, w):
    return x * lax.rsqrt(jnp.mean(x * x, axis=-1, keepdims=True) + EPS) * w


def _silu(x):
    return x * jax.nn.sigmoid(x)


def _softplus(x):
    return jnp.maximum(x, 0.0) + jnp.log(1.0 + jnp.exp(-jnp.abs(x)))


def _resident(shape):
    nd = len(shape)
    return pl.BlockSpec(shape, lambda *_: (0,) * nd, pipeline_mode=pl.Buffered(1))


def _ffn_kernel(x_ref, nw_ref, wup_ref, wdn_ref, fn_ref, o_ref, *, d_ff, n_split, final):
    x = x_ref[...]
    xn = _rms(x, nw_ref[...]).astype(BF16)
    step = d_ff // n_split
    y = None
    for c in range(n_split):
        g = _dot(xn, wup_ref[:, c * step:(c + 1) * step])
        u = _dot(xn, wup_ref[:, d_ff + c * step:d_ff + (c + 1) * step])
        a = (_silu(g) * u).astype(BF16)
        part = _dot(a, wdn_ref[c * step:(c + 1) * step, :])
        y = part if y is None else y + part
    h = x + 0.5 * y
    if final:
        h = _rms(h, fn_ref[...])
    o_ref[...] = h


def _ffn(h, nw, w_up, w_down, final_w, final):
    t, d = h.shape
    d_ff = w_down.shape[0]
    n_split = max(n for n in range(1, FFN_MAX_SPLIT + 1) if d_ff % (n * LANES) == 0)
    row = pl.BlockSpec((ROW_TILE, d), lambda i: (i, 0))
    return pl.pallas_call(
        functools.partial(_ffn_kernel, d_ff=d_ff, n_split=n_split, final=final),
        grid=(pl.cdiv(t, ROW_TILE),),
        in_specs=[row, _resident((1, d)), _resident(w_up.shape), _resident(w_down.shape),
                  _resident((1, d))],
        out_specs=row,
        out_shape=jax.ShapeDtypeStruct((t, d), F32),
        compiler_params=_params("parallel"),
        name="ffn",
    )(h, nw.reshape(1, d), w_up, w_down, final_w.reshape(1, d))


def _proj_kernel(x_ref, nw_ref, *refs):
    n = len(refs) // 2
    xn = _rms(x_ref[...], nw_ref[...]).astype(BF16)
    for w_ref, o_ref in zip(refs[:n], refs[n:]):
        o_ref[...] = _dot(xn, w_ref[...])


def _norm_proj(h, nw, weights):
    t, d = h.shape
    row = pl.BlockSpec((ROW_TILE, d), lambda i: (i, 0))
    return pl.pallas_call(
        _proj_kernel,
        grid=(pl.cdiv(t, ROW_TILE),),
        in_specs=[row, _resident((1, d))] + [_resident(w.shape) for w in weights],
        out_specs=[pl.BlockSpec((ROW_TILE, w.shape[1]), lambda i: (i, 0)) for w in weights],
        out_shape=[jax.ShapeDtypeStruct((t, w.shape[1]), F32) for w in weights],
        compiler_params=_params("parallel"),
        name="norm_proj",
    )(h, nw.reshape(1, d), *weights)


def _out_kernel(h_ref, *refs, n_o, gated, scale):
    o_refs = refs[:n_o]
    rest = refs[n_o:]
    if gated:
        gate_ref, on_ref, w_ref, out_ref = rest
    else:
        on_ref, w_ref, out_ref = rest
    o = o_refs[0][...].astype(F32)
    for r in o_refs[1:]:
        o = o + r[...].astype(F32)
    on = on_ref[...]
    parts = []
    for hd in range(HEADS):
        sl = slice(hd * HEAD_DIM, (hd + 1) * HEAD_DIM)
        y = _rms(o[:, sl], on)
        if gated:
            y = y * _silu(gate_ref[:, sl])
        else:
            y = y * scale
        parts.append(y.astype(BF16))
    y = jnp.concatenate(parts, axis=-1)
    out_ref[...] = h_ref[...] + _dot(y, w_ref[...])


def _out_proj(h, o_list, gate, gate_block, o_norm, w_out, scale=1.0):
    t, d = h.shape
    row = pl.BlockSpec((ROW_TILE, d), lambda i: (i, 0))
    in_specs = [row]
    args = [h]
    for arr, lead in o_list:
        if lead is None:
            in_specs.append(row)
        else:
            in_specs.append(pl.BlockSpec((None, ROW_TILE, d), lambda i, lead=lead: (lead, i, 0)))
        args.append(arr)
    gated = gate is not None
    if gated:
        in_specs.append(pl.BlockSpec((ROW_TILE, d), lambda i: (i, gate_block)))
        args.append(gate)
    in_specs += [_resident((1, HEAD_DIM)), _resident(w_out.shape)]
    args += [o_norm.reshape(1, HEAD_DIM), w_out]
    return pl.pallas_call(
        functools.partial(_out_kernel, n_o=len(o_list), gated=gated, scale=scale),
        grid=(pl.cdiv(t, ROW_TILE),),
        in_specs=in_specs,
        out_specs=row,
        out_shape=jax.ShapeDtypeStruct((t, d), F32),
        compiler_params=_params("parallel"),
        name="out_proj",
    )(*args)


def _conv_kernel(first_ref, xm_ref, xp_ref, xn_ref, w_ref, o_ref, buf_ref, *, tile):
    i = pl.program_id(0)
    j = pl.program_id(1)
    last = pl.num_programs(0) - 1
    buf_ref[8:8 + tile, :] = xm_ref[...]
    buf_ref[0:8, :] = jnp.where(i > 0, xp_ref[...], 0.0)
    buf_ref[8 + tile:16 + tile, :] = jnp.where(i < last, xn_ref[...], 0.0)
    half = (CONV_K - 1) // 2
    acc = None
    for k in range(CONV_K):
        term = w_ref[k:k + 1, :] * buf_ref[8 - half + k:8 - half + k + tile, :]
        acc = term if acc is None else acc + term
    norm_scale = jnp.where(j == 0, HEAD_DIM ** -0.5, 1.0)
    row = lax.broadcasted_iota(jnp.int32, (CHUNK, 1), 0)
    per_tile = tile // CHUNK
    for c in range(per_tile):
        n_dead = jnp.where(first_ref[i * per_tile + c] > 0, PAD, 0)
        y = jnp.where(row < n_dead, 0.0, acc[c * CHUNK:(c + 1) * CHUNK, :])
        y = _silu(y)
        for hd in range(HEADS):
            sl = slice(hd * HEAD_DIM, (hd + 1) * HEAD_DIM)
            blk = y[:, sl]
            inv = lax.rsqrt(jnp.sum(blk * blk, axis=-1, keepdims=True) + EPS) * norm_scale
            inv = jnp.where(j < 2, inv, 1.0)
            o_ref[c * CHUNK:(c + 1) * CHUNK, sl] = (blk * inv).astype(o_ref.dtype)


def _conv_qkv(proj, conv_w, first):
    t = proj.shape[0]
    width = HEADS * HEAD_DIM
    tile = CONV_TILE if t % CONV_TILE == 0 else CHUNK
    nb8 = t // 8
    tb8 = tile // 8
    grid_spec = pltpu.PrefetchScalarGridSpec(
        num_scalar_prefetch=1,
        grid=(t // tile, 3),
        in_specs=[
            pl.BlockSpec((tile, width), lambda i, j, f: (i, j)),
            pl.BlockSpec((8, width), lambda i, j, f: (jnp.maximum(i * tb8 - 1, 0), j)),
            pl.BlockSpec((8, width), lambda i, j, f: (jnp.minimum((i + 1) * tb8, nb8 - 1), j)),
            pl.BlockSpec((CONV_K, width), lambda i, j, f: (0, j)),
        ],
        out_specs=pl.BlockSpec((tile, width), lambda i, j, f: (i, j)),
        scratch_shapes=[pltpu.VMEM((tile + 16, width), F32)],
    )
    return pl.pallas_call(
        functools.partial(_conv_kernel, tile=tile),
        grid_spec=grid_spec,
        out_shape=jax.ShapeDtypeStruct((t, 3 * width), BF16),
        compiler_params=_params("arbitrary", "arbitrary"),
        name="conv_qkv",
    )(first, proj, proj, proj, conv_w)


def _scan_chunk(d, s, nc):
    return s if d == 0 else nc - 1 - s


def _order_masks(d):
    ii = lax.broadcasted_iota(jnp.int32, (CHUNK, CHUNK), 0)
    jj = lax.broadcasted_iota(jnp.int32, (CHUNK, CHUNK), 1)
    return (ii >= jj, ii > jj) if d == 0 else (ii <= jj, ii < jj)


def _delta_kernel(first_ref, last_ref, qf_ref, kf_ref, vf_ref, qb_ref, kb_ref, vb_ref,
                  bacf_ref, bacb_ref, barf_ref, barb_ref,
                  alog_r_ref, dtb_r_ref, alog_c_ref, dtb_c_ref, of_ref, ob_ref, s_ref, *, cps):
    s = pl.program_id(0)
    nb = pl.num_programs(0)
    block = (s, nb - 1 - s)

    @pl.when(s == 0)
    def _():
        s_ref[...] = jnp.zeros_like(s_ref)

    ii = lax.broadcasted_iota(jnp.int32, (CHUNK, CHUNK), 0)
    jj = lax.broadcasted_iota(jnp.int32, (CHUNK, CHUNK), 1)
    eye = jnp.where(ii == jj, 1.0, 0.0)
    n_lvl = int(math.log2(CHUNK))
    lvl = [((ii >> (b + 1)) == (jj >> (b + 1))) & ((ii >> b) != (jj >> b)) for b in range(n_lvl)]

    q_refs, k_refs, v_refs = (qf_ref, qb_ref), (kf_ref, kb_ref), (vf_ref, vb_ref)
    bac_refs, bar_refs, o_refs = (bacf_ref, bacb_ref), (barf_ref, barb_ref), (of_ref, ob_ref)
    incl, strict, m_incl3, m_incl = [], [], [], []
    for d in range(2):
        inc, strc = _order_masks(d)
        incl.append(inc)
        strict.append(strc)
        m_incl3.append(_order_mask3(d))
        m_incl.append(jnp.where(inc, 1.0, 0.0).astype(BF16))
    rows = [slice(c * CHUNK, (c + 1) * CHUNK) for c in range(cps)]
    beta_c, gc_c, gc_r, gt_c = {}, {}, {}, {}
    for d in range(2):
        for c in range(cps):
            bac = bac_refs[d][rows[c], :]
            bar = bar_refs[d][c]
            la_c = -jnp.exp(alog_r_ref[d]) * _softplus(bac[:, HEADS:] + dtb_r_ref[d])
            la_r = -jnp.exp(alog_c_ref[d]) * _softplus(bar[HEADS:, :] + dtb_c_ref[d])
            beta_c[d, c] = jax.nn.sigmoid(bac[:, :HEADS])
            gc_c[d, c] = _mask_dot(m_incl3[d], la_c)
            gc_r[d, c] = _dot_mask_nt(la_r, m_incl[d])
            gt_c[d, c] = jnp.sum(la_c, axis=0, keepdims=True)

    units = [(d, c, hd) for d in range(2) for c in range(cps) for hd in range(HEADS)]
    n = range(len(units))
    sls = [slice(hd * HEAD_DIM, (hd + 1) * HEAD_DIM) for _, _, hd in units]
    qs = [q_refs[d][rows[c], sls[x]] for x, (d, c, _) in enumerate(units)]
    ks = [k_refs[d][rows[c], sls[x]] for x, (d, c, _) in enumerate(units)]
    vs = [v_refs[d][rows[c], sls[x]] for x, (d, c, _) in enumerate(units)]
    g_col = [gc_c[d, c][:, hd:hd + 1] for d, c, hd in units]
    g_tot = [gt_c[d, c][:, hd:hd + 1] for d, c, hd in units]
    b_col = [beta_c[d, c][:, hd:hd + 1] for d, c, hd in units]
    decay = [jnp.exp(jnp.where(incl[d], g_col[x] - gc_r[d, c][hd:hd + 1, :], -jnp.inf))
             for x, (d, c, hd) in enumerate(units)]
    kb = [ks[x] * b_col[x] for x in n]
    k16 = [x.astype(BF16) for x in ks]
    a_mat = [jnp.where(strict[units[x][0]], _dot_nt(kb[x].astype(BF16), k16[x]) * decay[x], 0.0) for x in n]
    t_inv = [eye - jnp.where(lvl[0], a, 0.0) for a in a_mat]
    for b in range(1, n_lvl):
        t16 = [t.astype(BF16) for t in t_inv]
        a_off = [jnp.where(lvl[b], a, 0.0).astype(BF16) for a in a_mat]
        x16 = [_dot(a_off[x], t16[x]).astype(BF16) for x in n]
        t_inv = [t_inv[x] - _dot(t16[x], x16[x]) for x in n]
    t16 = [t.astype(BF16) for t in t_inv]
    e_col = [jnp.exp(g) for g in g_col]
    u = [_dot(t16[x], (vs[x] * b_col[x]).astype(BF16)) for x in n]
    w16 = [_dot(t16[x], (kb[x] * e_col[x]).astype(BF16)).astype(BF16) for x in n]
    qk16 = [(_dot_nt(qs[x].astype(BF16), k16[x]) * decay[x]).astype(BF16) for x in n]
    qd16 = [(qs[x] * e_col[x]).astype(BF16) for x in n]
    kd16 = [(ks[x] * jnp.exp(g_tot[x] - g_col[x])).astype(BF16) for x in n]
    a_last = [jnp.exp(g) for g in g_tot]

    index = {unit: x for x, unit in enumerate(units)}
    state = {(d, hd): s_ref[d, hd] for d in range(2) for hd in range(HEADS)}
    for t in range(cps):
        slot = (t, cps - 1 - t)
        fresh = (first_ref[block[0] * cps + slot[0]] > 0, last_ref[block[1] * cps + slot[1]] > 0)
        cur = [(d, slot[d], hd) for d in range(2) for hd in range(HEADS)]
        idx = [index[k] for k in cur]
        st = [jnp.where(fresh[d], 0.0, state[d, hd]) for d, _, hd in cur]
        st16 = [x.astype(BF16) for x in st]
        vn16 = [(u[idx[y]] - _dot(w16[idx[y]], st16[y])).astype(BF16) for y in range(len(cur))]
        for y, (d, c, _) in enumerate(cur):
            o_refs[d][rows[c], sls[idx[y]]] = (_dot(qd16[idx[y]], st16[y])
                                               + _dot(qk16[idx[y]], vn16[y])).astype(o_refs[d].dtype)
        for y, (d, _, hd) in enumerate(cur):
            state[d, hd] = st[y] * a_last[idx[y]] + _dot_tn(kd16[idx[y]], vn16[y])
    for (d, hd), val in state.items():
        s_ref[d, hd] = val


def _delta_rule(qkv, ba_c, ba_r, a_log, dt_bias, first, last):
    t = qkv.shape[0]
    nc = t // CHUNK
    cps = DELTA_CHUNKS_PER_STEP if nc % DELTA_CHUNKS_PER_STEP == 0 else 1
    nb = nc // cps
    width = HEADS * HEAD_DIM

    def col(j, d):
        return pl.BlockSpec((cps * CHUNK, width), lambda s, f, l: (_scan_chunk(d, s, nb), j))

    def bac(d):
        return pl.BlockSpec((None, cps * CHUNK, 2 * HEADS), lambda s, f, l: (d, _scan_chunk(d, s, nb), 0))

    def bar(d):
        return pl.BlockSpec((None, cps, 2 * HEADS, CHUNK), lambda s, f, l: (d, _scan_chunk(d, s, nb), 0, 0))

    par_r = pl.BlockSpec((2, 1, HEADS), lambda s, f, l: (0, 0, 0))
    par_c = pl.BlockSpec((2, HEADS, 1), lambda s, f, l: (0, 0, 0))
    grid_spec = pltpu.PrefetchScalarGridSpec(
        num_scalar_prefetch=2,
        grid=(nb,),
        in_specs=[col(0, 0), col(1, 0), col(2, 0), col(0, 1), col(1, 1), col(2, 1),
                  bac(0), bac(1), bar(0), bar(1), par_r, par_r, par_c, par_c],
        out_specs=[col(0, 0), col(0, 1)],
        scratch_shapes=[pltpu.VMEM((2, HEADS, HEAD_DIM, HEAD_DIM), F32)],
    )
    return pl.pallas_call(
        functools.partial(_delta_kernel, cps=cps),
        grid_spec=grid_spec,
        out_shape=[jax.ShapeDtypeStruct((t, width), BF16)] * 2,
        compiler_params=_params("arbitrary"),
        name="delta_rule",
    )(first, last, qkv, qkv, qkv, qkv, qkv, qkv, ba_c, ba_c, ba_r, ba_r,
      a_log.reshape(2, 1, HEADS), dt_bias.reshape(2, 1, HEADS),
      a_log.reshape(2, HEADS, 1), dt_bias.reshape(2, HEADS, 1))


def _gated_deltanet(h, nw, w_in, conv_w, a_log, dt_bias, o_norm, w_out, first, last):
    t = h.shape[0]
    width = HEADS * HEAD_DIM
    proj, ba = _norm_proj(h, nw, [w_in[:, :4 * width], w_in[:, 4 * width:]])
    qkv = _conv_qkv(proj, conv_w, first)
    ba = ba.reshape(t, 2, 2, HEADS)
    ba_c = jnp.transpose(ba, (2, 0, 1, 3)).reshape(2, t, 2 * HEADS)
    ba_r = jnp.transpose(ba_c.reshape(2, t // CHUNK, CHUNK, 2 * HEADS), (0, 1, 3, 2))
    o_f, o_b = _delta_rule(qkv, ba_c, ba_r, a_log, dt_bias, first, last)
    return _out_proj(h, [(o_f, None), (o_b, None)], proj, 3, o_norm, w_out)


def _gla_level_matrices():
    ii = np.arange(CHUNK)[:, None]
    jj = np.arange(CHUNK)[None, :]
    out = []
    for d in range(2):
        mats = [(ii >= jj) if d == 0 else (ii <= jj)]
        for b in range(int(math.log2(CHUNK))):
            size = 1 << b
            mid = ((ii >> (b + 1)) << (b + 1)) + size
            upper = (ii & size) != 0
            if d == 0:
                sel = (upper & (jj >= mid) & (jj <= ii)) | (~upper & (jj > ii) & (jj < mid))
            else:
                sel = (upper & (jj >= mid) & (jj < ii)) | (~upper & (jj >= ii) & (jj < mid))
            mats.append(sel)
        out.append(np.tile(np.concatenate(mats, axis=0), (1, 3)))
    return np.stack(out).astype(np.float32)


def _gla_kernel(first_ref, last_ref, qf_ref, vf_ref, ff_ref, qb_ref, vb_ref, fb_ref, lb_ref, lvl_ref,
                of_ref, ob_ref, s_ref, *, layer_idx):
    s = pl.program_id(0)
    nc = pl.num_programs(0)

    @pl.when(first_ref[s] > 0)
    def _():
        s_ref[0] = jnp.zeros_like(s_ref[0])

    @pl.when(last_ref[nc - 1 - s] > 0)
    def _():
        s_ref[1] = jnp.zeros_like(s_ref[1])

    ii = lax.broadcasted_iota(jnp.int32, (CHUNK, CHUNK), 0)
    jj = lax.broadcasted_iota(jnp.int32, (CHUNK, CHUNK), 1)
    n_lvl = int(math.log2(CHUNK))

    lbl = lb_ref[...]
    lbw = jnp.exp(lbl - jnp.max(lbl, axis=0, keepdims=True))
    lbw = lbw / jnp.sum(lbw, axis=0, keepdims=True)
    lb = jnp.zeros_like(lbw[0:1, :])
    for r in range(1, layer_idx + 1):
        lb = lb + lbw[r:r + 1, :]

    q_refs, v_refs, f_refs, o_refs = (qf_ref, qb_ref), (vf_ref, vb_ref), (ff_ref, fb_ref), (of_ref, ob_ref)
    pair_masks, q, kk, x_all, q_dec, k_dec, f_last = [], [], [], [], [], [], []
    for d in range(2):
        _, strict = _order_masks(d)
        pair_masks.append([((ii >> (b + 1)) == (jj >> (b + 1))) & ((ii >> b) != (jj >> b)) & strict
                           for b in range(n_lvl)])
        f = lb + (1.0 - lb) * jax.nn.sigmoid(f_refs[d][...])
        logf = jnp.log(f)
        e_all = _mask_dot(lvl_ref[d], logf)
        tot = jnp.sum(logf, axis=0, keepdims=True)
        xa = jnp.exp(e_all)
        qd = _silu(q_refs[d][...]) * (HEAD_DIM ** -0.5)
        q.append(qd)
        kk.append(1.0 - f)
        x_all.append(xa)
        q_dec.append(qd * xa[:CHUNK, :])
        k_dec.append((1.0 - f) * jnp.exp(tot - e_all[:CHUNK, :]))
        f_last.append(jnp.exp(tot))

    units = [(d, hd) for d in range(2) for hd in range(HEADS)]
    n = range(len(units))
    sls = [slice(hd * HEAD_DIM, (hd + 1) * HEAD_DIM) for _, hd in units]
    qs = [q[d][:, sls[x]] for x, (d, _) in enumerate(units)]
    ks = [kk[d][:, sls[x]] for x, (d, _) in enumerate(units)]
    attn = [jnp.where(ii == jj, _dot_nt(qs[x].astype(BF16), ks[x].astype(BF16)), 0.0) for x in n]
    for b in range(n_lvl):
        xb = [x_all[d][(b + 1) * CHUNK:(b + 2) * CHUNK, sls[x]] for x, (d, _) in enumerate(units)]
        part = [_dot_nt((qs[x] * xb[x]).astype(BF16), (ks[x] * xb[x]).astype(BF16)) for x in n]
        attn = [jnp.where(pair_masks[units[x][0]][b], part[x], attn[x]) for x in n]
    v16 = [v_refs[d][:, sls[x]].astype(BF16) for x, (d, _) in enumerate(units)]
    st_t = [s_ref[d, hd] for d, hd in units]
    for x, (d, _) in enumerate(units):
        o_refs[d][:, sls[x]] = (_dot_nt(q_dec[d][:, sls[x]].astype(BF16), st_t[x].astype(BF16))
                                + _dot(attn[x].astype(BF16), v16[x])).astype(o_refs[d].dtype)
    for x, (d, hd) in enumerate(units):
        s_ref[d, hd] = st_t[x] * f_last[d][:, sls[x]] + _dot_tn(v16[x], k_dec[d][:, sls[x]].astype(BF16))


def _gla(proj, lb_logits, layer_idx, first, last):
    t = proj.shape[0]
    nc = t // CHUNK
    width = HEADS * HEAD_DIM

    lvl = jnp.asarray(_gla_level_matrices(), BF16)

    def col(j, d):
        return pl.BlockSpec((CHUNK, width), lambda s, f, l: (_scan_chunk(d, s, nc), j))

    grid_spec = pltpu.PrefetchScalarGridSpec(
        num_scalar_prefetch=2,
        grid=(nc,),
        in_specs=[col(0, 0), col(1, 0), col(3, 0), col(0, 1), col(1, 1), col(4, 1),
                  pl.BlockSpec(lb_logits.shape, lambda s, f, l: (0, 0)),
                  pl.BlockSpec(lvl.shape, lambda s, f, l: (0, 0, 0))],
        out_specs=[col(0, 0), col(0, 1)],
        scratch_shapes=[pltpu.VMEM((2, HEADS, HEAD_DIM, HEAD_DIM), F32)],
    )
    return pl.pallas_call(
        functools.partial(_gla_kernel, layer_idx=layer_idx),
        grid_spec=grid_spec,
        out_shape=[jax.ShapeDtypeStruct((t, width), BF16)] * 2,
        compiler_params=_params("arbitrary"),
        name="gla",
    )(first, last, proj, proj, proj, proj, proj, proj, lb_logits, lvl)


def _hgrn2(h, nw, w_in, lb_logits, layer_idx, o_norm, w_out, first, last):
    (proj,) = _norm_proj(h, nw, [w_in])
    o_f, o_b = _gla(proj, lb_logits, layer_idx, first, last)
    return _out_proj(h, [(o_f, None), (o_b, None)], proj, 2, o_norm, w_out)


def _rope_kernel(x_ref, cos_ref, s1_ref, s2_ref, o_ref, *, n_rot_tiles, n_q_tiles):
    cos = cos_ref[...]
    s1 = s1_ref[...]
    s2 = s2_ref[...]
    for c in range(n_rot_tiles):
        sl = slice(c * LANES, (c + 1) * LANES)
        x = x_ref[:, sl]
        half = ATT_ROT // 2
        y = x * cos + pltpu.roll(x, LANES - half, 1) * s1 + pltpu.roll(x, half, 1) * s2
        if c < n_q_tiles:
            y = y * (ATT_HD ** -0.5 * LOG2E)
        o_ref[:, sl] = y.astype(BF16)
    o_ref[:, n_rot_tiles * LANES:] = x_ref[:, n_rot_tiles * LANES:].astype(BF16)


def _rope(proj, cos_t, s1_t, s2_t):
    t, n = proj.shape
    qk_w = HEADS * ATT_HD
    row = pl.BlockSpec((ATT_TQ, n), lambda i: (i, 0))
    tab = pl.BlockSpec((ATT_TQ, LANES), lambda i: (i, 0))
    return pl.pallas_call(
        functools.partial(_rope_kernel, n_rot_tiles=4 * qk_w // LANES, n_q_tiles=2 * qk_w // LANES),
        grid=(pl.cdiv(t, ATT_TQ),),
        in_specs=[row, tab, tab, tab],
        out_specs=row,
        out_shape=jax.ShapeDtypeStruct((t, n), BF16),
        compiler_params=_params("parallel"),
        name="rope",
    )(proj, cos_t, s1_t, s2_t)


def _attn_kernel(q1_ref, q2_ref, k1_ref, k2_ref, v_ref, lam_ref, o_ref, vt_ref, *scratch,
                 seq_len, lam_init):
    i = pl.program_id(2)
    n_full = seq_len // ATT_TK
    tail = seq_len - n_full * ATT_TK
    acc_refs, sc_refs = scratch[:4], scratch[4:]

    @pl.when(i == 0)
    def _():
        def body(j, carry):
            blk = v_ref[pl.ds(pl.multiple_of(j * ATT_TK, ATT_TK), ATT_TK), :]
            vt_ref[j] = blk.astype(F32).T.astype(BF16)
            return carry
        if n_full > 0:
            lax.fori_loop(0, n_full, body, 0)
        if tail > 0:
            blk = v_ref[n_full * ATT_TK:seq_len, :]
            vt_ref[n_full, :, :tail] = blk.astype(F32).T.astype(BF16)

    lane = lax.broadcasted_iota(jnp.int32, (1, LANES), 1)
    q_refs = (q1_ref, q2_ref)
    k_refs = (k1_ref, k2_ref)
    qm = []
    for hh in range(2):
        keep = (lane >= hh * ATT_HD) & (lane < (hh + 1) * ATT_HD)
        qm.append([jnp.where(keep, q_refs[m][...], jnp.zeros((), BF16)) for m in range(2)])

    for a in acc_refs:
        a[...] = jnp.zeros_like(a)
    units = [(hh, m) for hh in range(2) for m in range(2)]

    def score(u, j, size, masked):
        hh, m = units[u]
        start = j * ATT_TK
        if not isinstance(j, int):
            start = pl.multiple_of(start, ATT_TK)
        sc = _dot_nt(k_refs[m][pl.ds(start, size), :], qm[hh][m])
        if masked:
            rowid = lax.broadcasted_iota(jnp.int32, (size, 1), 0)
            sc = jnp.where(rowid >= PAD, sc, -jnp.inf)
        return sc

    def consume(u, sc, j, size, carry):
        hh = units[u][0]
        if size == ATT_TK:
            vt = vt_ref[j, hh * HEAD_DIM:(hh + 1) * HEAD_DIM, :]
        else:
            vt = vt_ref[j, hh * HEAD_DIM:(hh + 1) * HEAD_DIM, :size]
        m_old, l_old = carry[u], carry[4 + u]
        m_new = jnp.maximum(m_old, jnp.max(sc, axis=0, keepdims=True))
        alpha = jnp.exp2(m_old - m_new)
        p = jnp.exp2(sc - m_new)
        l_new = alpha * l_old + jnp.sum(p, axis=0, keepdims=True)
        acc_refs[u][...] = alpha * acc_refs[u][...] + _dot(vt, p.astype(BF16))
        return m_new, l_new

    def step(cur, j, size, carry, nxt=None):
        def issue(u):
            nxt[0][u][...] = score(u, nxt[1], ATT_TK, False)
        if nxt is not None:
            issue(0)
            issue(1)
        stats = []
        for u in range(4):
            stats.append(consume(u, cur[u](), j, size, carry))
            if nxt is not None and u + 2 < 4:
                issue(u + 2)
        return tuple(s[0] for s in stats) + tuple(s[1] for s in stats)

    def from_buf(buf):
        return [functools.partial(lambda r: r[...], buf[u]) for u in range(4)]

    def from_values(vals):
        return [functools.partial(lambda v: v, vals[u]) for u in range(4)]

    carry = (jnp.full((1, ATT_TQ), -jnp.inf, F32),) * 4 + (jnp.zeros((1, ATT_TQ), F32),) * 4
    bufs = (sc_refs[:4], sc_refs[4:])
    if n_full > 0:
        for u in range(4):
            bufs[0][u][...] = score(u, 0, ATT_TK, True)
        n_pairs = max(0, (n_full - 2) // 2)
        if n_pairs > 0:
            def body(i, c):
                c = step(from_buf(bufs[0]), 2 * i, ATT_TK, c, nxt=(bufs[1], 2 * i + 1))
                return step(from_buf(bufs[1]), 2 * i + 1, ATT_TK, c, nxt=(bufs[0], 2 * i + 2))
            carry = lax.fori_loop(0, n_pairs, body, carry)
        tail_sc = None
        for t in range(2 * n_pairs, n_full):
            if t + 1 < n_full:
                carry = step(from_buf(bufs[t % 2]), t, ATT_TK, carry, nxt=(bufs[(t + 1) % 2], t + 1))
            else:
                if tail > 0:
                    tail_sc = [score(u, n_full, tail, False) for u in range(4)]
                carry = step(from_buf(bufs[t % 2]), t, ATT_TK, carry)
        if tail > 0:
            carry = step(from_values(tail_sc), n_full, tail, carry)
    else:
        carry = step(from_values([score(u, 0, tail, True) for u in range(4)]), 0, tail, carry)
    l_fin = carry[4:]

    lv = lam_ref[...]
    lam = (jnp.exp(jnp.sum(lv[0:1] * lv[1:2], axis=-1, keepdims=True))
           - jnp.exp(jnp.sum(lv[2:3] * lv[3:4], axis=-1, keepdims=True)) + lam_init)
    row = i * ATT_TQ + lax.broadcasted_iota(jnp.int32, (ATT_TQ, 1), 0)
    for hh in range(2):
        o_t = (acc_refs[2 * hh][...] / l_fin[2 * hh]
               - lam * (acc_refs[2 * hh + 1][...] / l_fin[2 * hh + 1]))
        o_ref[:, hh * HEAD_DIM:(hh + 1) * HEAD_DIM] = jnp.where(row >= PAD, o_t.T, 0.0).astype(o_ref.dtype)


def _attention(qkv, lam_vec, lam_init):
    b, seq_len, _ = qkv.shape
    width = HEADS * HEAD_DIM
    n_pair = HEADS // 2
    pair_w = 2 * ATT_HD
    nq = pl.cdiv(seq_len, ATT_TQ)

    def qspec(off):
        return pl.BlockSpec((None, ATT_TQ, pair_w), lambda bb, p, i: (bb, i, off + p))

    def kspec(off):
        return pl.BlockSpec((None, seq_len, pair_w), lambda bb, p, i: (bb, 0, off + p))

    return pl.pallas_call(
        functools.partial(_attn_kernel, seq_len=seq_len, lam_init=lam_init),
        grid=(b, n_pair, nq),
        in_specs=[
            qspec(0), qspec(n_pair), kspec(2 * n_pair), kspec(3 * n_pair),
            pl.BlockSpec((None, seq_len, 2 * HEAD_DIM), lambda bb, p, i: (bb, 0, 2 * n_pair + p)),
            pl.BlockSpec(lam_vec.shape, lambda bb, p, i: (0, 0)),
        ],
        out_specs=pl.BlockSpec((None, ATT_TQ, 2 * HEAD_DIM), lambda bb, p, i: (bb, i, p)),
        out_shape=jax.ShapeDtypeStruct((b, seq_len, width), BF16),
        scratch_shapes=[pltpu.VMEM((pl.cdiv(seq_len, ATT_TK), 2 * HEAD_DIM, ATT_TK), BF16)]
        + [pltpu.VMEM((HEAD_DIM, ATT_TQ), F32)] * 4
        + [pltpu.VMEM((ATT_TK, ATT_TQ), F32)] * 8,
        compiler_params=_params("arbitrary", "arbitrary", "arbitrary"),
        name="diff_attention",
    )(qkv, qkv, qkv, qkv, qkv, lam_vec)


def _rope_tables(pos):
    half = ATT_ROT // 2
    inv_freq = jnp.exp(-math.log(ROPE_THETA) * jnp.arange(half, dtype=F32) / half)
    ang = pos.astype(F32)[:, None] * inv_freq
    cos, sin = jnp.cos(ang), jnp.sin(ang)
    n = pos.shape[0]
    rest = ATT_HD - ATT_ROT
    cos_t = jnp.concatenate([cos, cos, jnp.ones((n, rest), F32)], axis=1)
    s1_t = jnp.concatenate([-sin, jnp.zeros((n, ATT_HD - half), F32)], axis=1)
    s2_t = jnp.concatenate([jnp.zeros((n, half), F32), sin, jnp.zeros((n, rest), F32)], axis=1)
    rep = LANES // ATT_HD
    return tuple(jnp.tile(x, (1, rep)) for x in (cos_t, s1_t, s2_t))


def _diff_attention(h, nw, w_in, lam_vec, sub_norm, w_out, layer_idx, groups, pos):
    (proj,) = _norm_proj(h, nw, [w_in])
    roped = _rope(proj, *_rope_tables(pos))
    lam_init = 0.8 - 0.6 * math.exp(-0.3 * layer_idx)
    outs = []
    start = 0
    for b, seq_len in groups:
        rows = b * seq_len
        part = roped[start:start + rows].reshape(b, seq_len, roped.shape[1])
        outs.append(_attention(part, lam_vec, lam_init).reshape(rows, -1))
        start += rows
    o = jnp.concatenate(outs, axis=0)
    return _out_proj(h, [(o, None)], None, 0, sub_norm, w_out, scale=1.0 - lam_init)


def kernel(x_prompt, x_sample, meta_tokens, norm_w, ffn_w_up, ffn_w_down,
           a_w_in, a_conv_w, a_log, a_dt_bias, a_o_norm, a_w_out,
           b_w_in, b_lambda, b_sub_norm, b_w_out,
           c_w_in, c_lb_logits, c_o_norm, c_w_out, final_norm):
    d = x_prompt.shape[-1]
    depth = norm_w.shape[0]
    xs = (x_prompt, x_sample)
    groups = [(x.shape[0], x.shape[1] + CHUNK) for x in xs]

    parts = []
    for x in xs:
        b = x.shape[0]
        lead = jnp.concatenate([jnp.zeros((PAD, d), F32), meta_tokens.astype(F32)], axis=0)
        lead = jnp.broadcast_to(lead[None], (b, CHUNK, d))
        parts.append(jnp.concatenate([lead, x.astype(F32)], axis=1).reshape(-1, d))
    h = jnp.concatenate(parts, axis=0)
    t = h.shape[0]

    first = np.zeros((t // CHUNK,), np.int32)
    last = np.zeros((t // CHUNK,), np.int32)
    pos = np.zeros((t,), np.int32)
    start = 0
    for b, seq_len in groups:
        for _ in range(b):
            first[start // CHUNK] = 1
            last[(start + seq_len) // CHUNK - 1] = 1
            pos[start:start + seq_len] = np.maximum(np.arange(seq_len) - PAD, 0)
            start += seq_len
    first, last, pos = jnp.asarray(first), jnp.asarray(last), jnp.asarray(pos)

    w_up16 = ffn_w_up.astype(BF16)
    w_dn16 = ffn_w_down.astype(BF16)
    for i in range(depth):
        kind, j = i % N_MIXERS, i // N_MIXERS
        h = _ffn(h, norm_w[i, 0], w_up16[i, 0], w_dn16[i, 0], final_norm, False)
        if kind == 0:
            h = _gated_deltanet(h, norm_w[i, 1], a_w_in[j].astype(BF16), a_conv_w[j], a_log[j],
                                a_dt_bias[j], a_o_norm[j], a_w_out[j].astype(BF16), first, last)
        elif kind == 1:
            h = _diff_attention(h, norm_w[i, 1], b_w_in[j].astype(BF16), b_lambda[j], b_sub_norm[j],
                                b_w_out[j].astype(BF16), i, groups, pos)
        else:
            h = _hgrn2(h, norm_w[i, 1], c_w_in[j].astype(BF16), c_lb_logits, i, c_o_norm[j],
                       c_w_out[j].astype(BF16), first, last)
        h = _ffn(h, norm_w[i, 2], w_up16[i, 1], w_dn16[i, 1], final_norm, i == depth - 1)

    outs = []
    start = 0
    for (b, seq_len), x in zip(groups, xs):
        rows = b * seq_len
        outs.append(h[start:start + rows].reshape(b, seq_len, d)[:, CHUNK:].astype(x.dtype))
        start += rows
    return tuple(outs)
```

```python
import functools
import math

import numpy as np
import jax
import jax.numpy as jnp
from jax import lax
from jax.experimental import pallas as pl
from jax.experimental.pallas import tpu as pltpu

F32 = jnp.float32
BF16 = jnp.bfloat16

EPS = 1e-6
CHUNK = 64
N_META = 16
PAD = CHUNK - N_META
N_MIXERS = 3
CONV_K = 5
ROPE_THETA = 500000.0
HEADS = 8
HEAD_DIM = 128
ATT_HD = 64
ATT_ROT = ATT_HD // 4
LANES = 128
V7X_VMEM_LIMIT_BYTES = 56 * 1024 * 1024

ROW_TILE = 512
CONV_TILE = 448
ATT_TQ = 256
ATT_TK = 512
LOG2E = 1.4426950408889634
FFN_MAX_SPLIT = 11
SCAN_CHUNKS_PER_STEP = 2


def _params(*sem):
    return pltpu.CompilerParams(dimension_semantics=sem, vmem_limit_bytes=V7X_VMEM_LIMIT_BYTES)


def _dot(a, b):
    return jnp.dot(a, b, preferred_element_type=F32)


def _dot_nt(a, b):
    return lax.dot_general(a, b, (((1,), (1,)), ((), ())), preferred_element_type=F32)


def _dot_tn(a, b):
    return lax.dot_general(a, b, (((0,), (0,)), ((), ())), preferred_element_type=F32)


def _split3(x):
    hi = x.astype(BF16)
    r = x - hi.astype(F32)
    mid = r.astype(BF16)
    lo = (r - mid.astype(F32)).astype(BF16)
    return hi, mid, lo


def _mask_dot(mask3, x):
    return _dot(mask3, jnp.concatenate(_split3(x), axis=0))


def _dot_mask_nt(x, mask):
    hi, mid, lo = _split3(x)
    return _dot_nt(hi, mask) + _dot_nt(mid, mask) + _dot_nt(lo, mask)


def _order_mask3(d):
    ii = lax.broadcasted_iota(jnp.int32, (CHUNK, 3 * CHUNK), 0)
    ss = lax.broadcasted_iota(jnp.int32, (CHUNK, 3 * CHUNK), 1) & (CHUNK - 1)
    return jnp.where(ii >= ss if d == 0 else ii <= ss, 1.0, 0.0).astype(BF16)


def _rms(x, w):
    return x * lax.rsqrt(jnp.mean(x * x, axis=-1, keepdims=True) + EPS) * w


def _silu(x):
    return x * jax.nn.sigmoid(x)


def _softplus(x):
    return jnp.maximum(x, 0.0) + jnp.log(1.0 + jnp.exp(-jnp.abs(x)))


def _resident(shape):
    nd = len(shape)
    return pl.BlockSpec(shape, lambda *_: (0,) * nd, pipeline_mode=pl.Buffered(1))


def _ffn_kernel(x_ref, nw_ref, wup_ref, wdn_ref, fn_ref, o_ref, *, d_ff, n_split, final):
    x = x_ref[...]
    xn = _rms(x, nw_ref[...]).astype(BF16)
    step = d_ff // n_split
    y = None
    for c in range(n_split):
        g = _dot(xn, wup_ref[:, c * step:(c + 1) * step])
        u = _dot(xn, wup_ref[:, d_ff + c * step:d_ff + (c + 1) * step])
        a = (_silu(g) * u).astype(BF16)
        part = _dot(a, wdn_ref[c * step:(c + 1) * step, :])
        y = part if y is None else y + part
    h = x + 0.5 * y
    if final:
        h = _rms(h, fn_ref[...])
    o_ref[...] = h


def _ffn(h, nw, w_up, w_down, final_w, final):
    t, d = h.shape
    d_ff = w_down.shape[0]
    n_split = max(n for n in range(1, FFN_MAX_SPLIT + 1) if d_ff % (n * LANES) == 0)
    row = pl.BlockSpec((ROW_TILE, d), lambda i: (i, 0))
    return pl.pallas_call(
        functools.partial(_ffn_kernel, d_ff=d_ff, n_split=n_split, final=final),
        grid=(pl.cdiv(t, ROW_TILE),),
        in_specs=[row, _resident((1, d)), _resident(w_up.shape), _resident(w_down.shape),
                  _resident((1, d))],
        out_specs=row,
        out_shape=jax.ShapeDtypeStruct((t, d), F32),
        compiler_params=_params("parallel"),
        name="ffn",
    )(h, nw.reshape(1, d), w_up, w_down, final_w.reshape(1, d))


def _proj_kernel(x_ref, nw_ref, *refs):
    n = len(refs) // 2
    xn = _rms(x_ref[...], nw_ref[...]).astype(BF16)
    for w_ref, o_ref in zip(refs[:n], refs[n:]):
        o_ref[...] = _dot(xn, w_ref[...])


def _norm_proj(h, nw, weights):
    t, d = h.shape
    row = pl.BlockSpec((ROW_TILE, d), lambda i: (i, 0))
    return pl.pallas_call(
        _proj_kernel,
        grid=(pl.cdiv(t, ROW_TILE),),
        in_specs=[row, _resident((1, d))] + [_resident(w.shape) for w in weights],
        out_specs=[pl.BlockSpec((ROW_TILE, w.shape[1]), lambda i: (i, 0)) for w in weights],
        out_shape=[jax.ShapeDtypeStruct((t, w.shape[1]), F32) for w in weights],
        compiler_params=_params("parallel"),
        name="norm_proj",
    )(h, nw.reshape(1, d), *weights)


def _out_kernel(h_ref, *refs, n_o, gated, scale):
    o_refs = refs[:n_o]
    rest = refs[n_o:]
    if gated:
        gate_ref, on_ref, w_ref, out_ref = rest
    else:
        on_ref, w_ref, out_ref = rest
    o = o_refs[0][...].astype(F32)
    for r in o_refs[1:]:
        o = o + r[...].astype(F32)
    on = on_ref[...]
    parts = []
    for hd in range(HEADS):
        sl = slice(hd * HEAD_DIM, (hd + 1) * HEAD_DIM)
        y = _rms(o[:, sl], on)
        if gated:
            y = y * _silu(gate_ref[:, sl])
        else:
            y = y * scale
        parts.append(y.astype(BF16))
    y = jnp.concatenate(parts, axis=-1)
    out_ref[...] = h_ref[...] + _dot(y, w_ref[...])


def _out_proj(h, o_list, gate, gate_block, o_norm, w_out, scale=1.0):
    t, d = h.shape
    row = pl.BlockSpec((ROW_TILE, d), lambda i: (i, 0))
    in_specs = [row]
    args = [h]
    for arr, lead in o_list:
        if lead is None:
            in_specs.append(row)
        else:
            in_specs.append(pl.BlockSpec((None, ROW_TILE, d), lambda i, lead=lead: (lead, i, 0)))
        args.append(arr)
    gated = gate is not None
    if gated:
        in_specs.append(pl.BlockSpec((ROW_TILE, d), lambda i: (i, gate_block)))
        args.append(gate)
    in_specs += [_resident((1, HEAD_DIM)), _resident(w_out.shape)]
    args += [o_norm.reshape(1, HEAD_DIM), w_out]
    return pl.pallas_call(
        functools.partial(_out_kernel, n_o=len(o_list), gated=gated, scale=scale),
        grid=(pl.cdiv(t, ROW_TILE),),
        in_specs=in_specs,
        out_specs=row,
        out_shape=jax.ShapeDtypeStruct((t, d), F32),
        compiler_params=_params("parallel"),
        name="out_proj",
    )(*args)


CONV_HALO = 8
CONV_WIN = CHUNK + 2 * CONV_HALO


def _conv_shift_matrix():
    half = (CONV_K - 1) // 2
    taps = [k for k in range(CONV_K) if k != half]
    m = np.zeros((len(taps) * CHUNK, 2 * CONV_WIN), np.float32)
    for t, k in enumerate(taps):
        for i in range(CHUNK):
            src = i + CONV_HALO + k - half
            m[t * CHUNK + i, src] = 1.0
            m[t * CHUNK + i, CONV_WIN + src] = 1.0
    return m


def _conv_kernel(first_ref, xm_ref, xp_ref, xn_ref, w_ref, shift_ref, o_ref, buf_ref, *, tile):
    i = pl.program_id(0)
    j = pl.program_id(1)
    last = pl.num_programs(0) - 1
    buf_ref[CONV_HALO:CONV_HALO + tile, :] = xm_ref[...]
    buf_ref[0:CONV_HALO, :] = jnp.where(i > 0, xp_ref[...], 0.0)
    buf_ref[CONV_HALO + tile:2 * CONV_HALO + tile, :] = jnp.where(i < last, xn_ref[...], 0.0)
    half = (CONV_K - 1) // 2
    taps = [k for k in range(CONV_K) if k != half]
    norm_scale = jnp.where(j == 0, HEAD_DIM ** -0.5, 1.0)
    row = lax.broadcasted_iota(jnp.int32, (CHUNK, 1), 0)
    per_tile = tile // CHUNK
    for c in range(per_tile):
        win = buf_ref[c * CHUNK:c * CHUNK + CONV_WIN, :]
        hi = win.astype(BF16)
        lo = (win - hi.astype(F32)).astype(BF16)
        shifted = _dot(shift_ref[...], jnp.concatenate([hi, lo], axis=0))
        acc = w_ref[half:half + 1, :] * win[CONV_HALO:CONV_HALO + CHUNK, :]
        for t, k in enumerate(taps):
            acc = acc + w_ref[k:k + 1, :] * shifted[t * CHUNK:(t + 1) * CHUNK, :]
        n_dead = jnp.where(first_ref[i * per_tile + c] > 0, PAD, 0)
        y = jnp.where(row < n_dead, 0.0, acc)
        y = _silu(y)
        for hd in range(HEADS):
            sl = slice(hd * HEAD_DIM, (hd + 1) * HEAD_DIM)
            blk = y[:, sl]
            inv = lax.rsqrt(jnp.sum(blk * blk, axis=-1, keepdims=True) + EPS) * norm_scale
            inv = jnp.where(j < 2, inv, 1.0)
            o_ref[c * CHUNK:(c + 1) * CHUNK, sl] = (blk * inv).astype(o_ref.dtype)


def _conv_qkv(proj, conv_w, first):
    t = proj.shape[0]
    width = HEADS * HEAD_DIM
    tile = CONV_TILE if t % CONV_TILE == 0 else CHUNK
    n_halo = t // CONV_HALO
    tile_halo = tile // CONV_HALO
    shift = jnp.asarray(_conv_shift_matrix(), BF16)
    grid_spec = pltpu.PrefetchScalarGridSpec(
        num_scalar_prefetch=1,
        grid=(t // tile, 3),
        in_specs=[
            pl.BlockSpec((tile, width), lambda i, j, f: (i, j)),
            pl.BlockSpec((CONV_HALO, width), lambda i, j, f: (jnp.maximum(i * tile_halo - 1, 0), j)),
            pl.BlockSpec((CONV_HALO, width), lambda i, j, f: (jnp.minimum((i + 1) * tile_halo, n_halo - 1), j)),
            pl.BlockSpec((CONV_K, width), lambda i, j, f: (0, j)),
            pl.BlockSpec(shift.shape, lambda i, j, f: (0, 0)),
        ],
        out_specs=pl.BlockSpec((tile, width), lambda i, j, f: (i, j)),
        scratch_shapes=[pltpu.VMEM((tile + 2 * CONV_HALO, width), F32)],
    )
    return pl.pallas_call(
        functools.partial(_conv_kernel, tile=tile),
        grid_spec=grid_spec,
        out_shape=jax.ShapeDtypeStruct((t, 3 * width), BF16),
        compiler_params=_params("arbitrary", "arbitrary"),
        name="conv_qkv",
    )(first, proj, proj, proj, conv_w, shift)


def _scan_chunk(d, s, nc):
    return s if d == 0 else nc - 1 - s


def _order_masks(d):
    ii = lax.broadcasted_iota(jnp.int32, (CHUNK, CHUNK), 0)
    jj = lax.broadcasted_iota(jnp.int32, (CHUNK, CHUNK), 1)
    return (ii >= jj, ii > jj) if d == 0 else (ii <= jj, ii < jj)


def _delta_kernel(first_ref, last_ref, qf_ref, kf_ref, vf_ref, qb_ref, kb_ref, vb_ref,
                  bacf_ref, bacb_ref, barf_ref, barb_ref,
                  alog_r_ref, dtb_r_ref, alog_c_ref, dtb_c_ref, of_ref, ob_ref, s_ref, *, cps):
    s = pl.program_id(0)
    nb = pl.num_programs(0)
    block = (s, nb - 1 - s)

    @pl.when(s == 0)
    def _():
        s_ref[...] = jnp.zeros_like(s_ref)

    ii = lax.broadcasted_iota(jnp.int32, (CHUNK, CHUNK), 0)
    jj = lax.broadcasted_iota(jnp.int32, (CHUNK, CHUNK), 1)
    eye = jnp.where(ii == jj, 1.0, 0.0)
    n_lvl = int(math.log2(CHUNK))
    lvl = [((ii >> (b + 1)) == (jj >> (b + 1))) & ((ii >> b) != (jj >> b)) for b in range(n_lvl)]

    q_refs, k_refs, v_refs = (qf_ref, qb_ref), (kf_ref, kb_ref), (vf_ref, vb_ref)
    bac_refs, bar_refs, o_refs = (bacf_ref, bacb_ref), (barf_ref, barb_ref), (of_ref, ob_ref)
    incl, strict, m_incl3, m_incl = [], [], [], []
    for d in range(2):
        inc, strc = _order_masks(d)
        incl.append(inc)
        strict.append(strc)
        m_incl3.append(_order_mask3(d))
        m_incl.append(jnp.where(inc, 1.0, 0.0).astype(BF16))
    rows = [slice(c * CHUNK, (c + 1) * CHUNK) for c in range(cps)]
    beta_c, gc_c, gc_r, gt_c = {}, {}, {}, {}
    for d in range(2):
        for c in range(cps):
            bac = bac_refs[d][rows[c], :]
            bar = bar_refs[d][c]
            la_c = -jnp.exp(alog_r_ref[d]) * _softplus(bac[:, HEADS:] + dtb_r_ref[d])
            la_r = -jnp.exp(alog_c_ref[d]) * _softplus(bar[HEADS:, :] + dtb_c_ref[d])
            beta_c[d, c] = jax.nn.sigmoid(bac[:, :HEADS])
            gc_c[d, c] = _mask_dot(m_incl3[d], la_c)
            gc_r[d, c] = _dot_mask_nt(la_r, m_incl[d])
            gt_c[d, c] = jnp.sum(la_c, axis=0, keepdims=True)

    units = [(d, c, hd) for d in range(2) for c in range(cps) for hd in range(HEADS)]
    n = range(len(units))
    sls = [slice(hd * HEAD_DIM, (hd + 1) * HEAD_DIM) for _, _, hd in units]
    qs = [q_refs[d][rows[c], sls[x]] for x, (d, c, _) in enumerate(units)]
    ks = [k_refs[d][rows[c], sls[x]] for x, (d, c, _) in enumerate(units)]
    vs = [v_refs[d][rows[c], sls[x]] for x, (d, c, _) in enumerate(units)]
    g_col = [gc_c[d, c][:, hd:hd + 1] for d, c, hd in units]
    g_tot = [gt_c[d, c][:, hd:hd + 1] for d, c, hd in units]
    b_col = [beta_c[d, c][:, hd:hd + 1] for d, c, hd in units]
    decay = [jnp.exp(jnp.where(incl[d], g_col[x] - gc_r[d, c][hd:hd + 1, :], -jnp.inf))
             for x, (d, c, hd) in enumerate(units)]
    kb = [ks[x] * b_col[x] for x in n]
    k16 = [x.astype(BF16) for x in ks]
    a_mat = [jnp.where(strict[units[x][0]], _dot_nt(kb[x].astype(BF16), k16[x]) * decay[x], 0.0) for x in n]
    t_inv = [eye - jnp.where(lvl[0], a, 0.0) for a in a_mat]
    for b in range(1, n_lvl):
        t16 = [t.astype(BF16) for t in t_inv]
        a_off = [jnp.where(lvl[b], a, 0.0).astype(BF16) for a in a_mat]
        x16 = [_dot(a_off[x], t16[x]).astype(BF16) for x in n]
        t_inv = [t_inv[x] - _dot(t16[x], x16[x]) for x in n]
    t16 = [t.astype(BF16) for t in t_inv]
    e_col = [jnp.exp(g) for g in g_col]
    uw = [_dot(t16[x], jnp.concatenate([(vs[x] * b_col[x]).astype(BF16),
                                        (kb[x] * e_col[x]).astype(BF16)], axis=1)) for x in n]
    u = [y[:, :HEAD_DIM] for y in uw]
    w16 = [y[:, HEAD_DIM:].astype(BF16) for y in uw]
    qk16 = [(_dot_nt(qs[x].astype(BF16), k16[x]) * decay[x]).astype(BF16) for x in n]
    qd16 = [(qs[x] * e_col[x]).astype(BF16) for x in n]
    kd16 = [(ks[x] * jnp.exp(g_tot[x] - g_col[x])).astype(BF16) for x in n]
    a_last = [jnp.exp(g) for g in g_tot]

    index = {unit: x for x, unit in enumerate(units)}
    state = {(d, hd): s_ref[d, hd] for d in range(2) for hd in range(HEADS)}
    for t in range(cps):
        slot = (t, cps - 1 - t)
        fresh = (first_ref[block[0] * cps + slot[0]] > 0, last_ref[block[1] * cps + slot[1]] > 0)
        cur = [(d, slot[d], hd) for d in range(2) for hd in range(HEADS)]
        idx = [index[k] for k in cur]
        st = [jnp.where(fresh[d], 0.0, state[d, hd]) for d, _, hd in cur]
        st16 = [x.astype(BF16) for x in st]
        ws = [_dot(jnp.concatenate([w16[idx[y]], qd16[idx[y]]], axis=0), st16[y]) for y in range(len(cur))]
        vn16 = [(u[idx[y]] - ws[y][:CHUNK]).astype(BF16) for y in range(len(cur))]
        for y, (d, c, _) in enumerate(cur):
            o_refs[d][rows[c], sls[idx[y]]] = (ws[y][CHUNK:]
                                               + _dot(qk16[idx[y]], vn16[y])).astype(o_refs[d].dtype)
        for y, (d, _, hd) in enumerate(cur):
            state[d, hd] = st[y] * a_last[idx[y]] + _dot_tn(kd16[idx[y]], vn16[y])
    for (d, hd), val in state.items():
        s_ref[d, hd] = val


def _delta_rule(qkv, ba_c, ba_r, a_log, dt_bias, first, last):
    t = qkv.shape[0]
    nc = t // CHUNK
    cps = SCAN_CHUNKS_PER_STEP if nc % SCAN_CHUNKS_PER_STEP == 0 else 1
    nb = nc // cps
    width = HEADS * HEAD_DIM

    def col(j, d):
        return pl.BlockSpec((cps * CHUNK, width), lambda s, f, l: (_scan_chunk(d, s, nb), j))

    def bac(d):
        return pl.BlockSpec((None, cps * CHUNK, 2 * HEADS), lambda s, f, l: (d, _scan_chunk(d, s, nb), 0))

    def bar(d):
        return pl.BlockSpec((None, cps, 2 * HEADS, CHUNK), lambda s, f, l: (d, _scan_chunk(d, s, nb), 0, 0))

    par_r = pl.BlockSpec((2, 1, HEADS), lambda s, f, l: (0, 0, 0))
    par_c = pl.BlockSpec((2, HEADS, 1), lambda s, f, l: (0, 0, 0))
    grid_spec = pltpu.PrefetchScalarGridSpec(
        num_scalar_prefetch=2,
        grid=(nb,),
        in_specs=[col(0, 0), col(1, 0), col(2, 0), col(0, 1), col(1, 1), col(2, 1),
                  bac(0), bac(1), bar(0), bar(1), par_r, par_r, par_c, par_c],
        out_specs=[col(0, 0), col(0, 1)],
        scratch_shapes=[pltpu.VMEM((2, HEADS, HEAD_DIM, HEAD_DIM), F32)],
    )
    return pl.pallas_call(
        functools.partial(_delta_kernel, cps=cps),
        grid_spec=grid_spec,
        out_shape=[jax.ShapeDtypeStruct((t, width), BF16)] * 2,
        compiler_params=_params("arbitrary"),
        name="delta_rule",
    )(first, last, qkv, qkv, qkv, qkv, qkv, qkv, ba_c, ba_c, ba_r, ba_r,
      a_log.reshape(2, 1, HEADS), dt_bias.reshape(2, 1, HEADS),
      a_log.reshape(2, HEADS, 1), dt_bias.reshape(2, HEADS, 1))


def _gated_deltanet(h, nw, w_in, conv_w, a_log, dt_bias, o_norm, w_out, first, last):
    t = h.shape[0]
    width = HEADS * HEAD_DIM
    proj, ba = _norm_proj(h, nw, [w_in[:, :4 * width], w_in[:, 4 * width:]])
    qkv = _conv_qkv(proj, conv_w, first)
    ba = ba.reshape(t, 2, 2, HEADS)
    ba_c = jnp.transpose(ba, (2, 0, 1, 3)).reshape(2, t, 2 * HEADS)
    ba_r = jnp.transpose(ba_c.reshape(2, t // CHUNK, CHUNK, 2 * HEADS), (0, 1, 3, 2))
    o_f, o_b = _delta_rule(qkv, ba_c, ba_r, a_log, dt_bias, first, last)
    return _out_proj(h, [(o_f, None), (o_b, None)], proj, 3, o_norm, w_out)


def _gla_level_matrices():
    ii = np.arange(CHUNK)[:, None]
    jj = np.arange(CHUNK)[None, :]
    out = []
    for d in range(2):
        mats = [(ii >= jj) if d == 0 else (ii <= jj)]
        for b in range(int(math.log2(CHUNK))):
            size = 1 << b
            mid = ((ii >> (b + 1)) << (b + 1)) + size
            upper = (ii & size) != 0
            if d == 0:
                sel = (upper & (jj >= mid) & (jj <= ii)) | (~upper & (jj > ii) & (jj < mid))
            else:
                sel = (upper & (jj >= mid) & (jj < ii)) | (~upper & (jj >= ii) & (jj < mid))
            mats.append(sel)
        out.append(np.tile(np.concatenate(mats, axis=0), (1, 3)))
    return np.stack(out).astype(np.float32)


def _gla_kernel(first_ref, last_ref, qf_ref, vf_ref, ff_ref, qb_ref, vb_ref, fb_ref, lb_ref, lvl_ref,
                of_ref, ob_ref, s_ref, *, layer_idx, cps):
    s = pl.program_id(0)
    nb = pl.num_programs(0)
    block = (s, nb - 1 - s)

    @pl.when(s == 0)
    def _():
        s_ref[...] = jnp.zeros_like(s_ref)

    ii = lax.broadcasted_iota(jnp.int32, (CHUNK, CHUNK), 0)
    jj = lax.broadcasted_iota(jnp.int32, (CHUNK, CHUNK), 1)
    n_lvl = int(math.log2(CHUNK))

    lbl = lb_ref[...]
    lbw = jnp.exp(lbl - jnp.max(lbl, axis=0, keepdims=True))
    lbw = lbw / jnp.sum(lbw, axis=0, keepdims=True)
    lb = jnp.zeros_like(lbw[0:1, :])
    for r in range(1, layer_idx + 1):
        lb = lb + lbw[r:r + 1, :]

    q_refs, v_refs, f_refs, o_refs = (qf_ref, qb_ref), (vf_ref, vb_ref), (ff_ref, fb_ref), (of_ref, ob_ref)
    rows = [slice(c * CHUNK, (c + 1) * CHUNK) for c in range(cps)]
    pair_masks = []
    for d in range(2):
        _, strict = _order_masks(d)
        pair_masks.append([((ii >> (b + 1)) == (jj >> (b + 1))) & ((ii >> b) != (jj >> b)) & strict
                           for b in range(n_lvl)])
    q, kk, x_all, q_dec, k_dec, f_last = {}, {}, {}, {}, {}, {}
    for d in range(2):
        for c in range(cps):
            f = lb + (1.0 - lb) * jax.nn.sigmoid(f_refs[d][rows[c], :])
            logf = jnp.log(f)
            e_all = _mask_dot(lvl_ref[d], logf)
            tot = jnp.sum(logf, axis=0, keepdims=True)
            xa = jnp.exp(e_all)
            q[d, c] = _silu(q_refs[d][rows[c], :]) * (HEAD_DIM ** -0.5)
            kk[d, c] = 1.0 - f
            x_all[d, c] = xa.astype(BF16)
            q_dec[d, c] = q[d, c] * xa[:CHUNK, :]
            k_dec[d, c] = (1.0 - f) * jnp.exp(tot - e_all[:CHUNK, :])
            f_last[d, c] = jnp.exp(tot)

    units = [(d, c, hd) for d in range(2) for c in range(cps) for hd in range(HEADS)]
    n = range(len(units))
    sls = [slice(hd * HEAD_DIM, (hd + 1) * HEAD_DIM) for _, _, hd in units]
    q16 = [q[d, c][:, sls[x]].astype(BF16) for x, (d, c, _) in enumerate(units)]
    k16 = [kk[d, c][:, sls[x]].astype(BF16) for x, (d, c, _) in enumerate(units)]
    attn = [jnp.where(ii == jj, _dot_nt(q16[x], k16[x]), 0.0) for x in n]
    for b in range(n_lvl):
        xb = [x_all[d, c][(b + 1) * CHUNK:(b + 2) * CHUNK, sls[x]] for x, (d, c, _) in enumerate(units)]
        part = [_dot_nt(q16[x] * xb[x], k16[x] * xb[x]) for x in n]
        attn = [jnp.where(pair_masks[units[x][0]][b], part[x], attn[x]) for x in n]
    attn16 = [a.astype(BF16) for a in attn]
    v16 = [v_refs[d][rows[c], sls[x]].astype(BF16) for x, (d, c, _) in enumerate(units)]
    qd16 = [q_dec[d, c][:, sls[x]].astype(BF16) for x, (d, c, _) in enumerate(units)]
    kd16 = [k_dec[d, c][:, sls[x]].astype(BF16) for x, (d, c, _) in enumerate(units)]

    index = {unit: x for x, unit in enumerate(units)}
    state = {(d, hd): s_ref[d, hd] for d in range(2) for hd in range(HEADS)}
    for t in range(cps):
        slot = (t, cps - 1 - t)
        fresh = (first_ref[block[0] * cps + slot[0]] > 0, last_ref[block[1] * cps + slot[1]] > 0)
        cur = [(d, slot[d], hd) for d in range(2) for hd in range(HEADS)]
        idx = [index[k] for k in cur]
        st = [jnp.where(fresh[d], 0.0, state[d, hd]) for d, _, hd in cur]
        for y, (d, c, _) in enumerate(cur):
            o_refs[d][rows[c], sls[idx[y]]] = (_dot_nt(qd16[idx[y]], st[y].astype(BF16))
                                               + _dot(attn16[idx[y]], v16[idx[y]])).astype(o_refs[d].dtype)
        for y, (d, c, hd) in enumerate(cur):
            state[d, hd] = st[y] * f_last[d, c][:, sls[idx[y]]] + _dot_tn(v16[idx[y]], kd16[idx[y]])
    for (d, hd), val in state.items():
        s_ref[d, hd] = val


def _gla(proj, lb_logits, layer_idx, first, last):
    t = proj.shape[0]
    nc = t // CHUNK
    cps = SCAN_CHUNKS_PER_STEP if nc % SCAN_CHUNKS_PER_STEP == 0 else 1
    nb = nc // cps
    width = HEADS * HEAD_DIM

    lvl = jnp.asarray(_gla_level_matrices(), BF16)

    def col(j, d):
        return pl.BlockSpec((cps * CHUNK, width), lambda s, f, l: (_scan_chunk(d, s, nb), j))

    grid_spec = pltpu.PrefetchScalarGridSpec(
        num_scalar_prefetch=2,
        grid=(nb,),
        in_specs=[col(0, 0), col(1, 0), col(3, 0), col(0, 1), col(1, 1), col(4, 1),
                  pl.BlockSpec(lb_logits.shape, lambda s, f, l: (0, 0)),
                  pl.BlockSpec(lvl.shape, lambda s, f, l: (0, 0, 0))],
        out_specs=[col(0, 0), col(0, 1)],
        scratch_shapes=[pltpu.VMEM((2, HEADS, HEAD_DIM, HEAD_DIM), F32)],
    )
    return pl.pallas_call(
        functools.partial(_gla_kernel, layer_idx=layer_idx, cps=cps),
        grid_spec=grid_spec,
        out_shape=[jax.ShapeDtypeStruct((t, width), BF16)] * 2,
        compiler_params=_params("arbitrary"),
        name="gla",
    )(first, last, proj, proj, proj, proj, proj, proj, lb_logits, lvl)


def _hgrn2(h, nw, w_in, lb_logits, layer_idx, o_norm, w_out, first, last):
    (proj,) = _norm_proj(h, nw, [w_in])
    o_f, o_b = _gla(proj, lb_logits, layer_idx, first, last)
    return _out_proj(h, [(o_f, None), (o_b, None)], proj, 2, o_norm, w_out)


def _proj_rope_kernel(x_ref, nw_ref, w_ref, cos_ref, s1_ref, s2_ref, o_ref, *, n_rot_tiles, n_q_tiles):
    xn = _rms(x_ref[...], nw_ref[...]).astype(BF16)
    cos = cos_ref[...]
    s1 = s1_ref[...]
    s2 = s2_ref[...]
    half = ATT_ROT // 2
    for c2 in range(0, n_rot_tiles, 2):
        wide = _dot(xn, w_ref[:, c2 * LANES:(c2 + 2) * LANES])
        for c in (c2, c2 + 1):
            sl = slice(c * LANES, (c + 1) * LANES)
            x = wide[:, (c - c2) * LANES:(c - c2 + 1) * LANES]
            y = x * cos + pltpu.roll(x, LANES - half, 1) * s1 + pltpu.roll(x, half, 1) * s2
            if c < n_q_tiles:
                y = y * (ATT_HD ** -0.5 * LOG2E)
            o_ref[:, sl] = y.astype(BF16)
    rest = slice(n_rot_tiles * LANES, w_ref.shape[1])
    o_ref[:, rest] = _dot(xn, w_ref[:, rest]).astype(BF16)


def _norm_proj_rope(h, nw, w_in, cos_t, s1_t, s2_t):
    t, d = h.shape
    n = w_in.shape[1]
    qk_w = HEADS * ATT_HD
    row = pl.BlockSpec((ROW_TILE, d), lambda i: (i, 0))
    tab = pl.BlockSpec((ROW_TILE, LANES), lambda i: (i, 0))
    return pl.pallas_call(
        functools.partial(_proj_rope_kernel, n_rot_tiles=4 * qk_w // LANES, n_q_tiles=2 * qk_w // LANES),
        grid=(pl.cdiv(t, ROW_TILE),),
        in_specs=[row, _resident((1, d)), _resident(w_in.shape), tab, tab, tab],
        out_specs=pl.BlockSpec((ROW_TILE, n), lambda i: (i, 0)),
        out_shape=jax.ShapeDtypeStruct((t, n), BF16),
        compiler_params=_params("parallel"),
        name="norm_proj_rope",
    )(h, nw.reshape(1, d), w_in, cos_t, s1_t, s2_t)


def _attn_kernel(q1_ref, q2_ref, k1_ref, k2_ref, v_ref, lam_ref, o_ref, vt_ref, *scratch,
                 seq_len, lam_init):
    i = pl.program_id(2)
    n_full = seq_len // ATT_TK
    tail = seq_len - n_full * ATT_TK
    acc_refs, sc_refs = scratch[:4], scratch[4:]

    @pl.when(i == 0)
    def _():
        def body(j, carry):
            blk = v_ref[pl.ds(pl.multiple_of(j * ATT_TK, ATT_TK), ATT_TK), :]
            vt_ref[j] = blk.astype(F32).T.astype(BF16)
            return carry
        if n_full > 0:
            lax.fori_loop(0, n_full, body, 0)
        if tail > 0:
            blk = v_ref[n_full * ATT_TK:seq_len, :]
            vt_ref[n_full, :, :tail] = blk.astype(F32).T.astype(BF16)

    lane = lax.broadcasted_iota(jnp.int32, (1, LANES), 1)
    q_refs = (q1_ref, q2_ref)
    k_refs = (k1_ref, k2_ref)
    qm = []
    for hh in range(2):
        keep = (lane >= hh * ATT_HD) & (lane < (hh + 1) * ATT_HD)
        qm.append([jnp.where(keep, q_refs[m][...], jnp.zeros((), BF16)) for m in range(2)])

    for a in acc_refs:
        a[...] = jnp.zeros_like(a)
    units = [(hh, m) for hh in range(2) for m in range(2)]

    def score(u, j, size, masked):
        hh, m = units[u]
        start = j * ATT_TK
        if not isinstance(j, int):
            start = pl.multiple_of(start, ATT_TK)
        sc = _dot_nt(k_refs[m][pl.ds(start, size), :], qm[hh][m])
        if masked:
            rowid = lax.broadcasted_iota(jnp.int32, (size, 1), 0)
            sc = jnp.where(rowid >= PAD, sc, -jnp.inf)
        return sc

    def consume(u, sc, j, size, carry):
        hh = units[u][0]
        if size == ATT_TK:
            vt = vt_ref[j, hh * HEAD_DIM:(hh + 1) * HEAD_DIM, :]
        else:
            vt = vt_ref[j, hh * HEAD_DIM:(hh + 1) * HEAD_DIM, :size]
        m_old, l_old = carry[u], carry[4 + u]
        m_new = jnp.maximum(m_old, jnp.max(sc, axis=0, keepdims=True))
        alpha = jnp.exp2(m_old - m_new)
        p = jnp.exp2(sc - m_new)
        l_new = alpha * l_old + jnp.sum(p, axis=0, keepdims=True)
        acc_refs[u][...] = alpha * acc_refs[u][...] + _dot(vt, p.astype(BF16))
        return m_new, l_new

    def step(cur, j, size, carry, nxt=None):
        def issue(u):
            nxt[0][u][...] = score(u, nxt[1], ATT_TK, False)
        if nxt is not None:
            issue(0)
            issue(1)
        stats = []
        for u in range(4):
            stats.append(consume(u, cur[u](), j, size, carry))
            if nxt is not None and u + 2 < 4:
                issue(u + 2)
        return tuple(s[0] for s in stats) + tuple(s[1] for s in stats)

    def from_buf(buf):
        return [functools.partial(lambda r: r[...], buf[u]) for u in range(4)]

    def from_values(vals):
        return [functools.partial(lambda v: v, vals[u]) for u in range(4)]

    carry = (jnp.full((1, ATT_TQ), -jnp.inf, F32),) * 4 + (jnp.zeros((1, ATT_TQ), F32),) * 4
    bufs = (sc_refs[:4], sc_refs[4:])
    if n_full > 0:
        for u in range(4):
            bufs[0][u][...] = score(u, 0, ATT_TK, True)
        n_pairs = max(0, (n_full - 2) // 2)
        if n_pairs > 0:
            def body(i, c):
                c = step(from_buf(bufs[0]), 2 * i, ATT_TK, c, nxt=(bufs[1], 2 * i + 1))
                return step(from_buf(bufs[1]), 2 * i + 1, ATT_TK, c, nxt=(bufs[0], 2 * i + 2))
            carry = lax.fori_loop(0, n_pairs, body, carry)
        tail_sc = None
        for t in range(2 * n_pairs, n_full):
            if t + 1 < n_full:
                carry = step(from_buf(bufs[t % 2]), t, ATT_TK, carry, nxt=(bufs[(t + 1) % 2], t + 1))
            else:
                if tail > 0:
                    tail_sc = [score(u, n_full, tail, False) for u in range(4)]
                carry = step(from_buf(bufs[t % 2]), t, ATT_TK, carry)
        if tail > 0:
            carry = step(from_values(tail_sc), n_full, tail, carry)
    else:
        carry = step(from_values([score(u, 0, tail, True) for u in range(4)]), 0, tail, carry)
    l_fin = carry[4:]

    lv = lam_ref[...]
    lam = (jnp.exp(jnp.sum(lv[0:1] * lv[1:2], axis=-1, keepdims=True))
           - jnp.exp(jnp.sum(lv[2:3] * lv[3:4], axis=-1, keepdims=True)) + lam_init)
    row = i * ATT_TQ + lax.broadcasted_iota(jnp.int32, (ATT_TQ, 1), 0)
    for hh in range(2):
        o_t = (acc_refs[2 * hh][...] / l_fin[2 * hh]
               - lam * (acc_refs[2 * hh + 1][...] / l_fin[2 * hh + 1]))
        o_ref[:, hh * HEAD_DIM:(hh + 1) * HEAD_DIM] = jnp.where(row >= PAD, o_t.T, 0.0).astype(o_ref.dtype)


def _attention(qkv, lam_vec, lam_init):
    b, seq_len, _ = qkv.shape
    width = HEADS * HEAD_DIM
    n_pair = HEADS // 2
    pair_w = 2 * ATT_HD
    nq = pl.cdiv(seq_len, ATT_TQ)

    def qspec(off):
        return pl.BlockSpec((None, ATT_TQ, pair_w), lambda bb, p, i: (bb, i, off + p))

    def kspec(off):
        return pl.BlockSpec((None, seq_len, pair_w), lambda bb, p, i: (bb, 0, off + p))

    return pl.pallas_call(
        functools.partial(_attn_kernel, seq_len=seq_len, lam_init=lam_init),
        grid=(b, n_pair, nq),
        in_specs=[
            qspec(0), qspec(n_pair), kspec(2 * n_pair), kspec(3 * n_pair),
            pl.BlockSpec((None, seq_len, 2 * HEAD_DIM), lambda bb, p, i: (bb, 0, 2 * n_pair + p)),
            pl.BlockSpec(lam_vec.shape, lambda bb, p, i: (0, 0)),
        ],
        out_specs=pl.BlockSpec((None, ATT_TQ, 2 * HEAD_DIM), lambda bb, p, i: (bb, i, p)),
        out_shape=jax.ShapeDtypeStruct((b, seq_len, width), BF16),
        scratch_shapes=[pltpu.VMEM((pl.cdiv(seq_len, ATT_TK), 2 * HEAD_DIM, ATT_TK), BF16)]
        + [pltpu.VMEM((HEAD_DIM, ATT_TQ), F32)] * 4
        + [pltpu.VMEM((ATT_TK, ATT_TQ), F32)] * 8,
        compiler_params=_params("arbitrary", "arbitrary", "arbitrary"),
        name="diff_attention",
    )(qkv, qkv, qkv, qkv, qkv, lam_vec)


def _rope_tables(pos):
    half = ATT_ROT // 2
    inv_freq = jnp.exp(-math.log(ROPE_THETA) * jnp.arange(half, dtype=F32) / half)
    ang = pos.astype(F32)[:, None] * inv_freq
    cos, sin = jnp.cos(ang), jnp.sin(ang)
    n = pos.shape[0]
    rest = ATT_HD - ATT_ROT
    cos_t = jnp.concatenate([cos, cos, jnp.ones((n, rest), F32)], axis=1)
    s1_t = jnp.concatenate([-sin, jnp.zeros((n, ATT_HD - half), F32)], axis=1)
    s2_t = jnp.concatenate([jnp.zeros((n, half), F32), sin, jnp.zeros((n, rest), F32)], axis=1)
    rep = LANES // ATT_HD
    return tuple(jnp.tile(x, (1, rep)) for x in (cos_t, s1_t, s2_t))


def _diff_attention(h, nw, w_in, lam_vec, sub_norm, w_out, layer_idx, groups, pos):
    roped = _norm_proj_rope(h, nw, w_in, *_rope_tables(pos))
    lam_init = 0.8 - 0.6 * math.exp(-0.3 * layer_idx)
    outs = []
    start = 0
    for b, seq_len in groups:
        rows = b * seq_len
        part = roped[start:start + rows].reshape(b, seq_len, roped.shape[1])
        outs.append(_attention(part, lam_vec, lam_init).reshape(rows, -1))
        start += rows
    o = jnp.concatenate(outs, axis=0)
    return _out_proj(h, [(o, None)], None, 0, sub_norm, w_out, scale=1.0 - lam_init)


def kernel(x_prompt, x_sample, meta_tokens, norm_w, ffn_w_up, ffn_w_down,
           a_w_in, a_conv_w, a_log, a_dt_bias, a_o_norm, a_w_out,
           b_w_in, b_lambda, b_sub_norm, b_w_out,
           c_w_in, c_lb_logits, c_o_norm, c_w_out, final_norm):
    d = x_prompt.shape[-1]
    depth = norm_w.shape[0]
    xs = (x_prompt, x_sample)
    groups = [(x.shape[0], x.shape[1] + CHUNK) for x in xs]

    lead = jnp.concatenate([jnp.zeros((PAD, d), F32), meta_tokens.astype(F32)], axis=0)
    parts = []
    for x in xs:
        for b in range(x.shape[0]):
            parts += [lead, x[b].astype(F32)]
    h = jnp.concatenate(parts, axis=0)
    t = h.shape[0]

    first = np.zeros((t // CHUNK,), np.int32)
    last = np.zeros((t // CHUNK,), np.int32)
    pos = np.zeros((t,), np.int32)
    start = 0
    for b, seq_len in groups:
        for _ in range(b):
            first[start // CHUNK] = 1
            last[(start + seq_len) // CHUNK - 1] = 1
            pos[start:start + seq_len] = np.maximum(np.arange(seq_len) - PAD, 0)
            start += seq_len
    first, last, pos = jnp.asarray(first), jnp.asarray(last), jnp.asarray(pos)

    w_up16 = ffn_w_up.astype(BF16)
    w_dn16 = ffn_w_down.astype(BF16)
    for i in range(depth):
        kind, j = i % N_MIXERS, i // N_MIXERS
        h = _ffn(h, norm_w[i, 0], w_up16[i, 0], w_dn16[i, 0], final_norm, False)
        if kind == 0:
            h = _gated_deltanet(h, norm_w[i, 1], a_w_in[j].astype(BF16), a_conv_w[j], a_log[j],
                                a_dt_bias[j], a_o_norm[j], a_w_out[j].astype(BF16), first, last)
        elif kind == 1:
            h = _diff_attention(h, norm_w[i, 1], b_w_in[j].astype(BF16), b_lambda[j], b_sub_norm[j],
                                b_w_out[j].astype(BF16), i, groups, pos)
        else:
            h = _hgrn2(h, norm_w[i, 1], c_w_in[j].astype(BF16), c_lb_logits, i, c_o_norm[j],
                       c_w_out[j].astype(BF16), first, last)
        h = _ffn(h, norm_w[i, 2], w_up16[i, 1], w_dn16[i, 1], final_norm, i == depth - 1)

    outs = []
    start = 0
    for (b, seq_len), x in zip(groups, xs):
        rows = b * seq_len
        outs.append(h[start:start + rows].reshape(b, seq_len, d)[:, CHUNK:].astype(x.dtype))
        start += rows
    return tuple(outs)
```

```python
import functools
import math

import numpy as np
import jax
import jax.numpy as jnp
from jax import lax
from jax.experimental import pallas as pl
from jax.experimental.pallas import tpu as pltpu

F32 = jnp.float32
BF16 = jnp.bfloat16

EPS = 1e-6
CHUNK = 64
N_META = 16
PAD = CHUNK - N_META
N_MIXERS = 3
CONV_K = 5
ROPE_THETA = 500000.0
HEADS = 8
HEAD_DIM = 128
ATT_HD = 64
ATT_ROT = ATT_HD // 4
LANES = 128
V7X_VMEM_LIMIT_BYTES = 56 * 1024 * 1024

ROW_TILE = 512
WIDE_ROW_TILE = 896
CONV_TILE = 448
ATT_TQ = 256
ATT_TK = 512
LOG2E = 1.4426950408889634
FFN_MAX_SPLIT = 11
SCAN_CHUNKS_PER_STEP = 2


def _wide_row_tile(t):
    return WIDE_ROW_TILE if t % WIDE_ROW_TILE == 0 else ROW_TILE


def _params(*sem):
    return pltpu.CompilerParams(dimension_semantics=sem, vmem_limit_bytes=V7X_VMEM_LIMIT_BYTES)


def _dot(a, b):
    return jnp.dot(a, b, preferred_element_type=F32)


def _dot_nt(a, b):
    return lax.dot_general(a, b, (((1,), (1,)), ((), ())), preferred_element_type=F32)


def _dot_tn(a, b):
    return lax.dot_general(a, b, (((0,), (0,)), ((), ())), preferred_element_type=F32)


def _split3(x):
    hi = x.astype(BF16)
    r = x - hi.astype(F32)
    mid = r.astype(BF16)
    lo = (r - mid.astype(F32)).astype(BF16)
    return hi, mid, lo


def _mask_dot(mask3, x):
    return _dot(mask3, jnp.concatenate(_split3(x), axis=0))


def _dot_mask_nt(x, mask):
    hi, mid, lo = _split3(x)
    return _dot_nt(hi, mask) + _dot_nt(mid, mask) + _dot_nt(lo, mask)


def _order_mask3(d):
    ii = lax.broadcasted_iota(jnp.int32, (CHUNK, 3 * CHUNK), 0)
    ss = lax.broadcasted_iota(jnp.int32, (CHUNK, 3 * CHUNK), 1) & (CHUNK - 1)
    return jnp.where(ii >= ss if d == 0 else ii <= ss, 1.0, 0.0).astype(BF16)


def _rms(x, w):
    return x * lax.rsqrt(jnp.mean(x * x, axis=-1, keepdims=True) + EPS) * w


def _silu(x):
    return x * jax.nn.sigmoid(x)


def _softplus(x):
    return jnp.maximum(x, 0.0) + jnp.log(1.0 + jnp.exp(-jnp.abs(x)))


def _resident(shape):
    nd = len(shape)
    return pl.BlockSpec(shape, lambda *_: (0,) * nd, pipeline_mode=pl.Buffered(1))


def _ffn_kernel(x_ref, nw_ref, wup_ref, wdn_ref, fn_ref, o_ref, *, d_ff, n_split, final):
    x = x_ref[...]
    xn = _rms(x, nw_ref[...]).astype(BF16)
    step = d_ff // n_split
    y = None
    for c in range(n_split):
        g = _dot(xn, wup_ref[:, c * step:(c + 1) * step])
        u = _dot(xn, wup_ref[:, d_ff + c * step:d_ff + (c + 1) * step])
        a = (_silu(g) * u).astype(BF16)
        part = _dot(a, wdn_ref[c * step:(c + 1) * step, :])
        y = part if y is None else y + part
    h = x + 0.5 * y
    if final:
        h = _rms(h, fn_ref[...])
    o_ref[...] = h


def _ffn(h, nw, w_up, w_down, final_w, final):
    t, d = h.shape
    d_ff = w_down.shape[0]
    n_split = max(n for n in range(1, FFN_MAX_SPLIT + 1) if d_ff % (n * LANES) == 0)
    tile = _wide_row_tile(t)
    row = pl.BlockSpec((tile, d), lambda i: (i, 0))
    return pl.pallas_call(
        functools.partial(_ffn_kernel, d_ff=d_ff, n_split=n_split, final=final),
        grid=(pl.cdiv(t, tile),),
        in_specs=[row, _resident((1, d)), _resident(w_up.shape), _resident(w_down.shape),
                  _resident((1, d))],
        out_specs=row,
        out_shape=jax.ShapeDtypeStruct((t, d), F32),
        compiler_params=_params("parallel"),
        name="ffn",
    )(h, nw.reshape(1, d), w_up, w_down, final_w.reshape(1, d))


def _proj_kernel(x_ref, nw_ref, *refs):
    n = len(refs) // 2
    xn = _rms(x_ref[...], nw_ref[...]).astype(BF16)
    for w_ref, o_ref in zip(refs[:n], refs[n:]):
        o_ref[...] = _dot(xn, w_ref[...])


def _norm_proj(h, nw, weights):
    t, d = h.shape
    row = pl.BlockSpec((ROW_TILE, d), lambda i: (i, 0))
    return pl.pallas_call(
        _proj_kernel,
        grid=(pl.cdiv(t, ROW_TILE),),
        in_specs=[row, _resident((1, d))] + [_resident(w.shape) for w in weights],
        out_specs=[pl.BlockSpec((ROW_TILE, w.shape[1]), lambda i: (i, 0)) for w in weights],
        out_shape=[jax.ShapeDtypeStruct((t, w.shape[1]), F32) for w in weights],
        compiler_params=_params("parallel"),
        name="norm_proj",
    )(h, nw.reshape(1, d), *weights)


def _out_kernel(h_ref, *refs, n_o, gated, scale):
    o_refs = refs[:n_o]
    rest = refs[n_o:]
    if gated:
        gate_ref, on_ref, w_ref, out_ref = rest
    else:
        on_ref, w_ref, out_ref = rest
    o = o_refs[0][...].astype(F32)
    for r in o_refs[1:]:
        o = o + r[...].astype(F32)
    on = on_ref[...]
    parts = []
    for hd in range(HEADS):
        sl = slice(hd * HEAD_DIM, (hd + 1) * HEAD_DIM)
        y = _rms(o[:, sl], on)
        if gated:
            y = y * _silu(gate_ref[:, sl].astype(F32))
        else:
            y = y * scale
        parts.append(y.astype(BF16))
    y = jnp.concatenate(parts, axis=-1)
    out_ref[...] = h_ref[...] + _dot(y, w_ref[...])


def _out_proj(h, o_list, gate, gate_block, o_norm, w_out, scale=1.0):
    t, d = h.shape
    tile = _wide_row_tile(t)
    row = pl.BlockSpec((tile, d), lambda i: (i, 0))
    in_specs = [row]
    args = [h]
    for arr, lead in o_list:
        if lead is None:
            in_specs.append(row)
        else:
            in_specs.append(pl.BlockSpec((None, tile, d), lambda i, lead=lead: (lead, i, 0)))
        args.append(arr)
    gated = gate is not None
    if gated:
        in_specs.append(pl.BlockSpec((tile, d), lambda i: (i, gate_block)))
        args.append(gate)
    in_specs += [_resident((1, HEAD_DIM)), _resident(w_out.shape)]
    args += [o_norm.reshape(1, HEAD_DIM), w_out]
    return pl.pallas_call(
        functools.partial(_out_kernel, n_o=len(o_list), gated=gated, scale=scale),
        grid=(pl.cdiv(t, tile),),
        in_specs=in_specs,
        out_specs=row,
        out_shape=jax.ShapeDtypeStruct((t, d), F32),
        compiler_params=_params("parallel"),
        name="out_proj",
    )(*args)


CONV_HALO = 8


def _proj_conv_kernel(first_ref, xm_ref, xp_ref, xn_ref, nw_ref, wqkv_ref, wgate_ref, wba_ref, cw_ref,
                      qkv_ref, gate_ref, ba_ref, buf_ref, *, tile):
    i = pl.program_id(0)
    last = pl.num_programs(0) - 1
    width = HEADS * HEAD_DIM
    x_all = jnp.concatenate([jnp.where(i > 0, xp_ref[...], 0.0), xm_ref[...],
                             jnp.where(i < last, xn_ref[...], 0.0)], axis=0)
    xn_all = _rms(x_all, nw_ref[...]).astype(BF16)
    mid = slice(CONV_HALO, CONV_HALO + tile)
    half = (CONV_K - 1) // 2
    row = lax.broadcasted_iota(jnp.int32, (CHUNK, 1), 0)
    per_tile = tile // CHUNK

    def project(g):
        buf_ref[g] = _dot(xn_all, wqkv_ref[:, g * width:(g + 1) * width])

    def conv(g):
        cols = slice(g * width, (g + 1) * width)
        acc = None
        for k in range(CONV_K):
            lo = CONV_HALO - half + k
            term = cw_ref[k:k + 1, cols] * buf_ref[g, lo:lo + tile, :]
            acc = term if acc is None else acc + term
        for c in range(per_tile):
            n_dead = jnp.where(first_ref[i * per_tile + c] > 0, PAD, 0)
            y = jnp.where(row < n_dead, 0.0, acc[c * CHUNK:(c + 1) * CHUNK, :])
            y = _silu(y)
            for hd in range(HEADS):
                sl = slice(hd * HEAD_DIM, (hd + 1) * HEAD_DIM)
                blk = y[:, sl]
                if g < 2:
                    inv = lax.rsqrt(jnp.sum(blk * blk, axis=-1, keepdims=True) + EPS)
                    blk = blk * (inv * (HEAD_DIM ** -0.5) if g == 0 else inv)
                qkv_ref[c * CHUNK:(c + 1) * CHUNK, g * width + hd * HEAD_DIM:g * width + (hd + 1) * HEAD_DIM] = (
                    blk.astype(qkv_ref.dtype))

    project(0)
    project(1)
    conv(0)
    project(2)
    conv(1)
    gate_ref[...] = _dot(xn_all, wgate_ref[...])[mid].astype(gate_ref.dtype)
    ba_ref[...] = _dot(xn_all, wba_ref[...])[mid]
    conv(2)


def _norm_proj_conv(h, nw, w_in, conv_w, first):
    t, d = h.shape
    width = HEADS * HEAD_DIM
    n_ba = w_in.shape[1] - 4 * width
    tile = CONV_TILE if t % CONV_TILE == 0 else CHUNK
    n_halo = t // CONV_HALO
    tile_halo = tile // CONV_HALO

    def rows(n):
        return pl.BlockSpec((tile, n), lambda i, f: (i, 0))

    def const(shape):
        return pl.BlockSpec(shape, lambda i, f: (0,) * len(shape), pipeline_mode=pl.Buffered(1))

    grid_spec = pltpu.PrefetchScalarGridSpec(
        num_scalar_prefetch=1,
        grid=(t // tile,),
        in_specs=[
            rows(d),
            pl.BlockSpec((CONV_HALO, d), lambda i, f: (jnp.maximum(i * tile_halo - 1, 0), 0)),
            pl.BlockSpec((CONV_HALO, d), lambda i, f: (jnp.minimum((i + 1) * tile_halo, n_halo - 1), 0)),
            const((1, d)), const((d, 3 * width)), const((d, width)), const((d, n_ba)),
            const((CONV_K, 3 * width)),
        ],
        out_specs=[rows(3 * width), rows(width), rows(n_ba)],
        scratch_shapes=[pltpu.VMEM((3, tile + 2 * CONV_HALO, width), F32)],
    )
    return pl.pallas_call(
        functools.partial(_proj_conv_kernel, tile=tile),
        grid_spec=grid_spec,
        out_shape=[jax.ShapeDtypeStruct((t, 3 * width), BF16), jax.ShapeDtypeStruct((t, width), BF16),
                   jax.ShapeDtypeStruct((t, n_ba), F32)],
        compiler_params=_params("arbitrary"),
        name="norm_proj_conv",
    )(first, h, h, h, nw.reshape(1, d), w_in[:, :3 * width], w_in[:, 3 * width:4 * width],
      w_in[:, 4 * width:], conv_w)


def _scan_chunk(d, s, nc):
    return s if d == 0 else nc - 1 - s


def _order_masks(d):
    ii = lax.broadcasted_iota(jnp.int32, (CHUNK, CHUNK), 0)
    jj = lax.broadcasted_iota(jnp.int32, (CHUNK, CHUNK), 1)
    return (ii >= jj, ii > jj) if d == 0 else (ii <= jj, ii < jj)


def _delta_kernel(first_ref, last_ref, qf_ref, kf_ref, vf_ref, qb_ref, kb_ref, vb_ref,
                  bacf_ref, bacb_ref, barf_ref, barb_ref,
                  alog_r_ref, dtb_r_ref, alog_c_ref, dtb_c_ref, of_ref, ob_ref, s_ref, *, cps):
    s = pl.program_id(0)
    nb = pl.num_programs(0)
    block = (s, nb - 1 - s)

    @pl.when(s == 0)
    def _():
        s_ref[...] = jnp.zeros_like(s_ref)

    ii = lax.broadcasted_iota(jnp.int32, (CHUNK, CHUNK), 0)
    jj = lax.broadcasted_iota(jnp.int32, (CHUNK, CHUNK), 1)
    eye = jnp.where(ii == jj, 1.0, 0.0)
    n_lvl = int(math.log2(CHUNK))
    lvl = [((ii >> (b + 1)) == (jj >> (b + 1))) & ((ii >> b) != (jj >> b)) for b in range(n_lvl)]

    q_refs, k_refs, v_refs = (qf_ref, qb_ref), (kf_ref, kb_ref), (vf_ref, vb_ref)
    bac_refs, bar_refs, o_refs = (bacf_ref, bacb_ref), (barf_ref, barb_ref), (of_ref, ob_ref)
    incl, strict, m_incl3, m_incl = [], [], [], []
    for d in range(2):
        inc, strc = _order_masks(d)
        incl.append(inc)
        strict.append(strc)
        m_incl3.append(_order_mask3(d))
        m_incl.append(jnp.where(inc, 1.0, 0.0).astype(BF16))
    rows = [slice(c * CHUNK, (c + 1) * CHUNK) for c in range(cps)]
    beta_c, gc_c, gc_r, gt_c = {}, {}, {}, {}
    for d in range(2):
        for c in range(cps):
            bac = bac_refs[d][rows[c], :]
            bar = bar_refs[d][c]
            la_c = -jnp.exp(alog_r_ref[d]) * _softplus(bac[:, HEADS:] + dtb_r_ref[d])
            la_r = -jnp.exp(alog_c_ref[d]) * _softplus(bar[HEADS:, :] + dtb_c_ref[d])
            beta_c[d, c] = jax.nn.sigmoid(bac[:, :HEADS])
            gc_c[d, c] = _mask_dot(m_incl3[d], la_c)
            gc_r[d, c] = _dot_mask_nt(la_r, m_incl[d])
            gt_c[d, c] = jnp.sum(la_c, axis=0, keepdims=True)

    units = [(d, c, hd) for d in range(2) for c in range(cps) for hd in range(HEADS)]
    n = range(len(units))
    sls = [slice(hd * HEAD_DIM, (hd + 1) * HEAD_DIM) for _, _, hd in units]
    qs = [q_refs[d][rows[c], sls[x]] for x, (d, c, _) in enumerate(units)]
    ks = [k_refs[d][rows[c], sls[x]] for x, (d, c, _) in enumerate(units)]
    vs = [v_refs[d][rows[c], sls[x]] for x, (d, c, _) in enumerate(units)]
    g_col = [gc_c[d, c][:, hd:hd + 1] for d, c, hd in units]
    g_tot = [gt_c[d, c][:, hd:hd + 1] for d, c, hd in units]
    b_col = [beta_c[d, c][:, hd:hd + 1] for d, c, hd in units]
    decay = [jnp.exp(jnp.where(incl[d], g_col[x] - gc_r[d, c][hd:hd + 1, :], -jnp.inf))
             for x, (d, c, hd) in enumerate(units)]
    kb = [ks[x] * b_col[x] for x in n]
    k16 = [x.astype(BF16) for x in ks]
    a_mat = [jnp.where(strict[units[x][0]], _dot_nt(kb[x].astype(BF16), k16[x]) * decay[x], 0.0) for x in n]
    t_inv = [eye - jnp.where(lvl[0], a, 0.0) for a in a_mat]
    for b in range(1, n_lvl):
        t16 = [t.astype(BF16) for t in t_inv]
        a_off = [jnp.where(lvl[b], a, 0.0).astype(BF16) for a in a_mat]
        x16 = [_dot(a_off[x], t16[x]).astype(BF16) for x in n]
        t_inv = [t_inv[x] - _dot(t16[x], x16[x]) for x in n]
    t16 = [t.astype(BF16) for t in t_inv]
    e_col = [jnp.exp(g) for g in g_col]
    uw = [_dot(t16[x], jnp.concatenate([(vs[x] * b_col[x]).astype(BF16),
                                        (kb[x] * e_col[x]).astype(BF16)], axis=1)) for x in n]
    u = [y[:, :HEAD_DIM] for y in uw]
    w16 = [y[:, HEAD_DIM:].astype(BF16) for y in uw]
    qk16 = [(_dot_nt(qs[x].astype(BF16), k16[x]) * decay[x]).astype(BF16) for x in n]
    qd16 = [(qs[x] * e_col[x]).astype(BF16) for x in n]
    kd16 = [(ks[x] * jnp.exp(g_tot[x] - g_col[x])).astype(BF16) for x in n]
    a_last = [jnp.exp(g) for g in g_tot]

    index = {unit: x for x, unit in enumerate(units)}
    state = {(d, hd): s_ref[d, hd] for d in range(2) for hd in range(HEADS)}
    for t in range(cps):
        slot = (t, cps - 1 - t)
        fresh = (first_ref[block[0] * cps + slot[0]] > 0, last_ref[block[1] * cps + slot[1]] > 0)
        cur = [(d, slot[d], hd) for d in range(2) for hd in range(HEADS)]
        idx = [index[k] for k in cur]
        st = [jnp.where(fresh[d], 0.0, state[d, hd]) for d, _, hd in cur]
        st16 = [x.astype(BF16) for x in st]
        ws = [_dot(jnp.concatenate([w16[idx[y]], qd16[idx[y]]], axis=0), st16[y]) for y in range(len(cur))]
        vn16 = [(u[idx[y]] - ws[y][:CHUNK]).astype(BF16) for y in range(len(cur))]
        for y, (d, c, _) in enumerate(cur):
            o_refs[d][rows[c], sls[idx[y]]] = (ws[y][CHUNK:]
                                               + _dot(qk16[idx[y]], vn16[y])).astype(o_refs[d].dtype)
        for y, (d, _, hd) in enumerate(cur):
            state[d, hd] = st[y] * a_last[idx[y]] + _dot_tn(kd16[idx[y]], vn16[y])
    for (d, hd), val in state.items():
        s_ref[d, hd] = val


def _delta_rule(qkv, ba_c, ba_r, a_log, dt_bias, first, last):
    t = qkv.shape[0]
    nc = t // CHUNK
    cps = SCAN_CHUNKS_PER_STEP if nc % SCAN_CHUNKS_PER_STEP == 0 else 1
    nb = nc // cps
    width = HEADS * HEAD_DIM

    def col(j, d):
        return pl.BlockSpec((cps * CHUNK, width), lambda s, f, l: (_scan_chunk(d, s, nb), j))

    def bac(d):
        return pl.BlockSpec((None, cps * CHUNK, 2 * HEADS), lambda s, f, l: (d, _scan_chunk(d, s, nb), 0))

    def bar(d):
        return pl.BlockSpec((None, cps, 2 * HEADS, CHUNK), lambda s, f, l: (d, _scan_chunk(d, s, nb), 0, 0))

    par_r = pl.BlockSpec((2, 1, HEADS), lambda s, f, l: (0, 0, 0))
    par_c = pl.BlockSpec((2, HEADS, 1), lambda s, f, l: (0, 0, 0))
    grid_spec = pltpu.PrefetchScalarGridSpec(
        num_scalar_prefetch=2,
        grid=(nb,),
        in_specs=[col(0, 0), col(1, 0), col(2, 0), col(0, 1), col(1, 1), col(2, 1),
                  bac(0), bac(1), bar(0), bar(1), par_r, par_r, par_c, par_c],
        out_specs=[col(0, 0), col(0, 1)],
        scratch_shapes=[pltpu.VMEM((2, HEADS, HEAD_DIM, HEAD_DIM), F32)],
    )
    return pl.pallas_call(
        functools.partial(_delta_kernel, cps=cps),
        grid_spec=grid_spec,
        out_shape=[jax.ShapeDtypeStruct((t, width), BF16)] * 2,
        compiler_params=_params("arbitrary"),
        name="delta_rule",
    )(first, last, qkv, qkv, qkv, qkv, qkv, qkv, ba_c, ba_c, ba_r, ba_r,
      a_log.reshape(2, 1, HEADS), dt_bias.reshape(2, 1, HEADS),
      a_log.reshape(2, HEADS, 1), dt_bias.reshape(2, HEADS, 1))


def _gated_deltanet(h, nw, w_in, conv_w, a_log, dt_bias, o_norm, w_out, first, last):
    t = h.shape[0]
    width = HEADS * HEAD_DIM
    qkv, gate, ba = _norm_proj_conv(h, nw, w_in, conv_w, first)
    ba = ba.reshape(t, 2, 2, HEADS)
    ba_c = jnp.transpose(ba, (2, 0, 1, 3)).reshape(2, t, 2 * HEADS)
    ba_r = jnp.transpose(ba_c.reshape(2, t // CHUNK, CHUNK, 2 * HEADS), (0, 1, 3, 2))
    o_f, o_b = _delta_rule(qkv, ba_c, ba_r, a_log, dt_bias, first, last)
    return _out_proj(h, [(o_f, None), (o_b, None)], gate, 0, o_norm, w_out)


def _gla_level_matrices():
    ii = np.arange(CHUNK)[:, None]
    jj = np.arange(CHUNK)[None, :]
    out = []
    for d in range(2):
        mats = [(ii >= jj) if d == 0 else (ii <= jj)]
        for b in range(int(math.log2(CHUNK))):
            size = 1 << b
            mid = ((ii >> (b + 1)) << (b + 1)) + size
            upper = (ii & size) != 0
            if d == 0:
                sel = (upper & (jj >= mid) & (jj <= ii)) | (~upper & (jj > ii) & (jj < mid))
            else:
                sel = (upper & (jj >= mid) & (jj < ii)) | (~upper & (jj >= ii) & (jj < mid))
            mats.append(sel)
        out.append(np.tile(np.concatenate(mats, axis=0), (1, 3)))
    return np.stack(out).astype(np.float32)


def _gla_kernel(first_ref, last_ref, qf_ref, vf_ref, ff_ref, qb_ref, vb_ref, fb_ref, lb_ref, lvl_ref,
                of_ref, ob_ref, s_ref, *, layer_idx, cps):
    s = pl.program_id(0)
    nb = pl.num_programs(0)
    block = (s, nb - 1 - s)

    @pl.when(s == 0)
    def _():
        s_ref[...] = jnp.zeros_like(s_ref)

    ii = lax.broadcasted_iota(jnp.int32, (CHUNK, CHUNK), 0)
    jj = lax.broadcasted_iota(jnp.int32, (CHUNK, CHUNK), 1)
    n_lvl = int(math.log2(CHUNK))

    lbl = lb_ref[...]
    lbw = jnp.exp(lbl - jnp.max(lbl, axis=0, keepdims=True))
    lbw = lbw / jnp.sum(lbw, axis=0, keepdims=True)
    lb = jnp.zeros_like(lbw[0:1, :])
    for r in range(1, layer_idx + 1):
        lb = lb + lbw[r:r + 1, :]

    q_refs, v_refs, f_refs, o_refs = (qf_ref, qb_ref), (vf_ref, vb_ref), (ff_ref, fb_ref), (of_ref, ob_ref)
    rows = [slice(c * CHUNK, (c + 1) * CHUNK) for c in range(cps)]
    pair_masks = []
    for d in range(2):
        _, strict = _order_masks(d)
        pair_masks.append([((ii >> (b + 1)) == (jj >> (b + 1))) & ((ii >> b) != (jj >> b)) & strict
                           for b in range(n_lvl)])
    q, kk, x_all, q_dec, k_dec, f_last = {}, {}, {}, {}, {}, {}
    for d in range(2):
        for c in range(cps):
            f = lb + (1.0 - lb) * jax.nn.sigmoid(f_refs[d][rows[c], :])
            logf = jnp.log(f)
            e_all = _mask_dot(lvl_ref[d], logf)
            tot = jnp.sum(logf, axis=0, keepdims=True)
            xa = jnp.exp(e_all)
            q[d, c] = _silu(q_refs[d][rows[c], :]) * (HEAD_DIM ** -0.5)
            kk[d, c] = 1.0 - f
            x_all[d, c] = xa.astype(BF16)
            q_dec[d, c] = q[d, c] * xa[:CHUNK, :]
            k_dec[d, c] = (1.0 - f) * jnp.exp(tot - e_all[:CHUNK, :])
            f_last[d, c] = jnp.exp(tot)

    units = [(d, c, hd) for d in range(2) for c in range(cps) for hd in range(HEADS)]
    n = range(len(units))
    sls = [slice(hd * HEAD_DIM, (hd + 1) * HEAD_DIM) for _, _, hd in units]
    q16 = [q[d, c][:, sls[x]].astype(BF16) for x, (d, c, _) in enumerate(units)]
    k16 = [kk[d, c][:, sls[x]].astype(BF16) for x, (d, c, _) in enumerate(units)]
    attn = [jnp.where(ii == jj, _dot_nt(q16[x], k16[x]), 0.0) for x in n]
    for b in range(n_lvl):
        xb = [x_all[d, c][(b + 1) * CHUNK:(b + 2) * CHUNK, sls[x]] for x, (d, c, _) in enumerate(units)]
        part = [_dot_nt(q16[x] * xb[x], k16[x] * xb[x]) for x in n]
        attn = [jnp.where(pair_masks[units[x][0]][b], part[x], attn[x]) for x in n]
    attn16 = [a.astype(BF16) for a in attn]
    v16 = [v_refs[d][rows[c], sls[x]].astype(BF16) for x, (d, c, _) in enumerate(units)]
    qd16 = [q_dec[d, c][:, sls[x]].astype(BF16) for x, (d, c, _) in enumerate(units)]
    kd16 = [k_dec[d, c][:, sls[x]].astype(BF16) for x, (d, c, _) in enumerate(units)]

    index = {unit: x for x, unit in enumerate(units)}
    state = {(d, hd): s_ref[d, hd] for d in range(2) for hd in range(HEADS)}
    for t in range(cps):
        slot = (t, cps - 1 - t)
        fresh = (first_ref[block[0] * cps + slot[0]] > 0, last_ref[block[1] * cps + slot[1]] > 0)
        cur = [(d, slot[d], hd) for d in range(2) for hd in range(HEADS)]
        idx = [index[k] for k in cur]
        st = [jnp.where(fresh[d], 0.0, state[d, hd]) for d, _, hd in cur]
        for y, (d, c, _) in enumerate(cur):
            o_refs[d][rows[c], sls[idx[y]]] = (_dot_nt(qd16[idx[y]], st[y].astype(BF16))
                                               + _dot(attn16[idx[y]], v16[idx[y]])).astype(o_refs[d].dtype)
        for y, (d, c, hd) in enumerate(cur):
            state[d, hd] = st[y] * f_last[d, c][:, sls[idx[y]]] + _dot_tn(v16[idx[y]], kd16[idx[y]])
    for (d, hd), val in state.items():
        s_ref[d, hd] = val


def _gla(proj, lb_logits, layer_idx, first, last):
    t = proj.shape[0]
    nc = t // CHUNK
    cps = SCAN_CHUNKS_PER_STEP if nc % SCAN_CHUNKS_PER_STEP == 0 else 1
    nb = nc // cps
    width = HEADS * HEAD_DIM

    lvl = jnp.asarray(_gla_level_matrices(), BF16)

    def col(j, d):
        return pl.BlockSpec((cps * CHUNK, width), lambda s, f, l: (_scan_chunk(d, s, nb), j))

    grid_spec = pltpu.PrefetchScalarGridSpec(
        num_scalar_prefetch=2,
        grid=(nb,),
        in_specs=[col(0, 0), col(1, 0), col(3, 0), col(0, 1), col(1, 1), col(4, 1),
                  pl.BlockSpec(lb_logits.shape, lambda s, f, l: (0, 0)),
                  pl.BlockSpec(lvl.shape, lambda s, f, l: (0, 0, 0))],
        out_specs=[col(0, 0), col(0, 1)],
        scratch_shapes=[pltpu.VMEM((2, HEADS, HEAD_DIM, HEAD_DIM), F32)],
    )
    return pl.pallas_call(
        functools.partial(_gla_kernel, layer_idx=layer_idx, cps=cps),
        grid_spec=grid_spec,
        out_shape=[jax.ShapeDtypeStruct((t, width), BF16)] * 2,
        compiler_params=_params("arbitrary"),
        name="gla",
    )(first, last, proj, proj, proj, proj, proj, proj, lb_logits, lvl)


def _hgrn2(h, nw, w_in, lb_logits, layer_idx, o_norm, w_out, first, last):
    (proj,) = _norm_proj(h, nw, [w_in])
    o_f, o_b = _gla(proj, lb_logits, layer_idx, first, last)
    return _out_proj(h, [(o_f, None), (o_b, None)], proj, 2, o_norm, w_out)


def _proj_rope_kernel(x_ref, nw_ref, w_ref, cos_ref, s1_ref, s2_ref, o_ref, *, n_rot_tiles, n_q_tiles):
    xn = _rms(x_ref[...], nw_ref[...]).astype(BF16)
    cos = cos_ref[...]
    s1 = s1_ref[...]
    s2 = s2_ref[...]
    half = ATT_ROT // 2
    for c2 in range(0, n_rot_tiles, 2):
        wide = _dot(xn, w_ref[:, c2 * LANES:(c2 + 2) * LANES])
        for c in (c2, c2 + 1):
            sl = slice(c * LANES, (c + 1) * LANES)
            x = wide[:, (c - c2) * LANES:(c - c2 + 1) * LANES]
            y = x * cos + pltpu.roll(x, LANES - half, 1) * s1 + pltpu.roll(x, half, 1) * s2
            if c < n_q_tiles:
                y = y * (ATT_HD ** -0.5 * LOG2E)
            o_ref[:, sl] = y.astype(BF16)
    rest = slice(n_rot_tiles * LANES, w_ref.shape[1])
    o_ref[:, rest] = _dot(xn, w_ref[:, rest]).astype(BF16)


def _norm_proj_rope(h, nw, w_in, cos_t, s1_t, s2_t):
    t, d = h.shape
    n = w_in.shape[1]
    qk_w = HEADS * ATT_HD
    row = pl.BlockSpec((ROW_TILE, d), lambda i: (i, 0))
    tab = pl.BlockSpec((ROW_TILE, LANES), lambda i: (i, 0))
    return pl.pallas_call(
        functools.partial(_proj_rope_kernel, n_rot_tiles=4 * qk_w // LANES, n_q_tiles=2 * qk_w // LANES),
        grid=(pl.cdiv(t, ROW_TILE),),
        in_specs=[row, _resident((1, d)), _resident(w_in.shape), tab, tab, tab],
        out_specs=pl.BlockSpec((ROW_TILE, n), lambda i: (i, 0)),
        out_shape=jax.ShapeDtypeStruct((t, n), BF16),
        compiler_params=_params("parallel"),
        name="norm_proj_rope",
    )(h, nw.reshape(1, d), w_in, cos_t, s1_t, s2_t)


def _key_tiles(seq_len):
    n = max(1, seq_len // ATT_TK)
    return [ATT_TK] * (n - 1) + [seq_len - (n - 1) * ATT_TK]


def _attn_kernel(q1_ref, q2_ref, k1_ref, k2_ref, v_ref, lam_ref, o_ref, vt_ref, *scratch,
                 seq_len, lam_init):
    i = pl.program_id(2)
    sizes = _key_tiles(seq_len)
    n_tiles = len(sizes)
    acc_refs, sc_refs = scratch[:4], scratch[4:]

    @pl.when(i == 0)
    def _():
        def body(j, carry):
            blk = v_ref[pl.ds(pl.multiple_of(j * ATT_TK, ATT_TK), ATT_TK), :]
            vt_ref[j, :, :ATT_TK] = blk.astype(F32).T.astype(BF16)
            return carry
        if n_tiles > 1:
            lax.fori_loop(0, n_tiles - 1, body, 0)
        blk = v_ref[(n_tiles - 1) * ATT_TK:seq_len, :]
        vt_ref[n_tiles - 1, :, :sizes[-1]] = blk.astype(F32).T.astype(BF16)

    lane = lax.broadcasted_iota(jnp.int32, (1, LANES), 1)
    q_refs = (q1_ref, q2_ref)
    k_refs = (k1_ref, k2_ref)
    qm = []
    for hh in range(2):
        keep = (lane >= hh * ATT_HD) & (lane < (hh + 1) * ATT_HD)
        qm.append([jnp.where(keep, q_refs[m][...], jnp.zeros((), BF16)) for m in range(2)])

    for a in acc_refs:
        a[...] = jnp.zeros_like(a)
    units = [(hh, m) for hh in range(2) for m in range(2)]

    def score(u, j, size, masked):
        hh, m = units[u]
        start = j * ATT_TK
        if not isinstance(j, int):
            start = pl.multiple_of(start, ATT_TK)
        sc = _dot_nt(k_refs[m][pl.ds(start, size), :], qm[hh][m])
        if masked:
            rowid = lax.broadcasted_iota(jnp.int32, (size, 1), 0)
            sc = jnp.where(rowid >= PAD, sc, -jnp.inf)
        return sc

    def consume(u, sc, j, size, carry):
        hh = units[u][0]
        vt = vt_ref[j, hh * HEAD_DIM:(hh + 1) * HEAD_DIM, :size]
        m_old, l_old = carry[u], carry[4 + u]
        m_new = jnp.maximum(m_old, jnp.max(sc, axis=0, keepdims=True))
        alpha = jnp.exp2(m_old - m_new)
        p = jnp.exp2(sc - m_new)
        l_new = alpha * l_old + jnp.sum(p, axis=0, keepdims=True)
        acc_refs[u][...] = alpha * acc_refs[u][...] + _dot(vt, p.astype(BF16))
        return m_new, l_new

    bufs = (sc_refs[:4], sc_refs[4:])

    def step(j, size, carry, cur, nxt=None, size_n=None):
        def issue(u):
            nxt[u][:size_n, :] = score(u, j + 1, size_n, False)
        if nxt is not None:
            issue(0)
            issue(1)
        stats = []
        for u in range(4):
            stats.append(consume(u, cur[u][:size, :], j, size, carry))
            if nxt is not None and u + 2 < 4:
                issue(u + 2)
        return tuple(s[0] for s in stats) + tuple(s[1] for s in stats)

    carry = (jnp.full((1, ATT_TQ), -jnp.inf, F32),) * 4 + (jnp.zeros((1, ATT_TQ), F32),) * 4
    for u in range(4):
        bufs[0][u][:sizes[0], :] = score(u, 0, sizes[0], True)
    n_pairs = max(0, (n_tiles - 2) // 2)
    if n_pairs > 0:
        def body(i, c):
            c = step(2 * i, ATT_TK, c, bufs[0], bufs[1], ATT_TK)
            return step(2 * i + 1, ATT_TK, c, bufs[1], bufs[0], ATT_TK)
        carry = lax.fori_loop(0, n_pairs, body, carry)
    for t in range(2 * n_pairs, n_tiles):
        if t + 1 < n_tiles:
            carry = step(t, sizes[t], carry, bufs[t % 2], bufs[(t + 1) % 2], sizes[t + 1])
        else:
            carry = step(t, sizes[t], carry, bufs[t % 2])
    l_fin = carry[4:]

    lv = lam_ref[...]
    lam = (jnp.exp(jnp.sum(lv[0:1] * lv[1:2], axis=-1, keepdims=True))
           - jnp.exp(jnp.sum(lv[2:3] * lv[3:4], axis=-1, keepdims=True)) + lam_init)
    row = i * ATT_TQ + lax.broadcasted_iota(jnp.int32, (ATT_TQ, 1), 0)
    for hh in range(2):
        o_t = (acc_refs[2 * hh][...] / l_fin[2 * hh]
               - lam * (acc_refs[2 * hh + 1][...] / l_fin[2 * hh + 1]))
        o_ref[:, hh * HEAD_DIM:(hh + 1) * HEAD_DIM] = jnp.where(row >= PAD, o_t.T, 0.0).astype(o_ref.dtype)


def _attention(qkv, lam_vec, lam_init):
    b, seq_len, _ = qkv.shape
    width = HEADS * HEAD_DIM
    n_pair = HEADS // 2
    pair_w = 2 * ATT_HD
    nq = pl.cdiv(seq_len, ATT_TQ)
    sizes = _key_tiles(seq_len)

    def qspec(off):
        return pl.BlockSpec((None, ATT_TQ, pair_w), lambda bb, p, i: (bb, i, off + p))

    def kspec(off):
        return pl.BlockSpec((None, seq_len, pair_w), lambda bb, p, i: (bb, 0, off + p))

    return pl.pallas_call(
        functools.partial(_attn_kernel, seq_len=seq_len, lam_init=lam_init),
        grid=(b, n_pair, nq),
        in_specs=[
            qspec(0), qspec(n_pair), kspec(2 * n_pair), kspec(3 * n_pair),
            pl.BlockSpec((None, seq_len, 2 * HEAD_DIM), lambda bb, p, i: (bb, 0, 2 * n_pair + p)),
            pl.BlockSpec(lam_vec.shape, lambda bb, p, i: (0, 0)),
        ],
        out_specs=pl.BlockSpec((None, ATT_TQ, 2 * HEAD_DIM), lambda bb, p, i: (bb, i, p)),
        out_shape=jax.ShapeDtypeStruct((b, seq_len, width), BF16),
        scratch_shapes=[pltpu.VMEM((len(sizes), 2 * HEAD_DIM, max(sizes)), BF16)]
        + [pltpu.VMEM((HEAD_DIM, ATT_TQ), F32)] * 4
        + [pltpu.VMEM((max(sizes), ATT_TQ), F32)] * 8,
        compiler_params=_params("arbitrary", "arbitrary", "arbitrary"),
        name="diff_attention",
    )(qkv, qkv, qkv, qkv, qkv, lam_vec)


def _rope_tables(pos):
    half = ATT_ROT // 2
    inv_freq = jnp.exp(-math.log(ROPE_THETA) * jnp.arange(half, dtype=F32) / half)
    ang = pos.astype(F32)[:, None] * inv_freq
    cos, sin = jnp.cos(ang), jnp.sin(ang)
    n = pos.shape[0]
    rest = ATT_HD - ATT_ROT
    cos_t = jnp.concatenate([cos, cos, jnp.ones((n, rest), F32)], axis=1)
    s1_t = jnp.concatenate([-sin, jnp.zeros((n, ATT_HD - half), F32)], axis=1)
    s2_t = jnp.concatenate([jnp.zeros((n, half), F32), sin, jnp.zeros((n, rest), F32)], axis=1)
    rep = LANES // ATT_HD
    return tuple(jnp.tile(x, (1, rep)) for x in (cos_t, s1_t, s2_t))


def _diff_attention(h, nw, w_in, lam_vec, sub_norm, w_out, layer_idx, groups, pos):
    roped = _norm_proj_rope(h, nw, w_in, *_rope_tables(pos))
    lam_init = 0.8 - 0.6 * math.exp(-0.3 * layer_idx)
    outs = []
    start = 0
    for b, seq_len in groups:
        rows = b * seq_len
        part = roped[start:start + rows].reshape(b, seq_len, roped.shape[1])
        outs.append(_attention(part, lam_vec, lam_init).reshape(rows, -1))
        start += rows
    o = jnp.concatenate(outs, axis=0)
    return _out_proj(h, [(o, None)], None, 0, sub_norm, w_out, scale=1.0 - lam_init)


def kernel(x_prompt, x_sample, meta_tokens, norm_w, ffn_w_up, ffn_w_down,
           a_w_in, a_conv_w, a_log, a_dt_bias, a_o_norm, a_w_out,
           b_w_in, b_lambda, b_sub_norm, b_w_out,
           c_w_in, c_lb_logits, c_o_norm, c_w_out, final_norm):
    d = x_prompt.shape[-1]
    depth = norm_w.shape[0]
    xs = (x_prompt, x_sample)
    groups = [(x.shape[0], x.shape[1] + CHUNK) for x in xs]

    lead = jnp.concatenate([jnp.zeros((PAD, d), F32), meta_tokens.astype(F32)], axis=0)
    parts = []
    for x in xs:
        for b in range(x.shape[0]):
            parts += [lead, x[b].astype(F32)]
    h = jnp.concatenate(parts, axis=0)
    t = h.shape[0]

    first = np.zeros((t // CHUNK,), np.int32)
    last = np.zeros((t // CHUNK,), np.int32)
    pos = np.zeros((t,), np.int32)
    start = 0
    for b, seq_len in groups:
        for _ in range(b):
            first[start // CHUNK] = 1
            last[(start + seq_len) // CHUNK - 1] = 1
            pos[start:start + seq_len] = np.maximum(np.arange(seq_len) - PAD, 0)
            start += seq_len
    first, last, pos = jnp.asarray(first), jnp.asarray(last), jnp.asarray(pos)

    w_up16 = ffn_w_up.astype(BF16)
    w_dn16 = ffn_w_down.astype(BF16)
    for i in range(depth):
        kind, j = i % N_MIXERS, i // N_MIXERS
        h = _ffn(h, norm_w[i, 0], w_up16[i, 0], w_dn16[i, 0], final_norm, False)
        if kind == 0:
            h = _gated_deltanet(h, norm_w[i, 1], a_w_in[j].astype(BF16), a_conv_w[j], a_log[j],
                                a_dt_bias[j], a_o_norm[j], a_w_out[j].astype(BF16), first, last)
        elif kind == 1:
            h = _diff_attention(h, norm_w[i, 1], b_w_in[j].astype(BF16), b_lambda[j], b_sub_norm[j],
                                b_w_out[j].astype(BF16), i, groups, pos)
        else:
            h = _hgrn2(h, norm_w[i, 1], c_w_in[j].astype(BF16), c_lb_logits, i, c_o_norm[j],
                       c_w_out[j].astype(BF16), first, last)
        h = _ffn(h, norm_w[i, 2], w_up16[i, 1], w_dn16[i, 1], final_norm, i == depth - 1)

    outs = []
    start = 0
    for (b, seq_len), x in zip(groups, xs):
        rows = b * seq_len
        outs.append(h[start:start + rows].reshape(b, seq_len, d)[:, CHUNK:].astype(x.dtype))
        start += rows
    return tuple(outs)
```

```python
import functools
import math

import numpy as np
import jax
import jax.numpy as jnp
from jax import lax
from jax.experimental import pallas as pl
from jax.experimental.pallas import tpu as pltpu

F32 = jnp.float32
BF16 = jnp.bfloat16

EPS = 1e-6
CHUNK = 64
N_META = 16
PAD = CHUNK - N_META
N_MIXERS = 3
CONV_K = 5
ROPE_THETA = 500000.0
HEADS = 8
HEAD_DIM = 128
ATT_HD = 64
ATT_ROT = ATT_HD // 4
LANES = 128
V7X_VMEM_LIMIT_BYTES = 56 * 1024 * 1024

ROW_TILE = 512
WIDE_ROW_TILE = 896
CONV_TILE = 448
ATT_TQ = 256
ATT_TK = 512
LOG2E = 1.4426950408889634
LOG_GATE_FLOOR = -1e30
FFN_MAX_SPLIT = 11
SCAN_CHUNKS_PER_STEP = 2


def _wide_row_tile(t):
    return WIDE_ROW_TILE if t % WIDE_ROW_TILE == 0 else ROW_TILE


def _params(*sem):
    return pltpu.CompilerParams(dimension_semantics=sem, vmem_limit_bytes=V7X_VMEM_LIMIT_BYTES)


def _dot(a, b):
    return jnp.dot(a, b, preferred_element_type=F32)


def _dot_nt(a, b):
    return lax.dot_general(a, b, (((1,), (1,)), ((), ())), preferred_element_type=F32)


def _dot_tn(a, b):
    return lax.dot_general(a, b, (((0,), (0,)), ((), ())), preferred_element_type=F32)


def _split3(x):
    hi = x.astype(BF16)
    r = x - hi.astype(F32)
    mid = r.astype(BF16)
    lo = (r - mid.astype(F32)).astype(BF16)
    return hi, mid, lo


def _mask_dot(mask3, x):
    return _dot(mask3, jnp.concatenate(_split3(x), axis=0))


def _dot_mask_nt(x, mask):
    hi, mid, lo = _split3(x)
    return _dot_nt(hi, mask) + _dot_nt(mid, mask) + _dot_nt(lo, mask)


def _order_mask3(d):
    ii = lax.broadcasted_iota(jnp.int32, (CHUNK, 3 * CHUNK), 0)
    ss = lax.broadcasted_iota(jnp.int32, (CHUNK, 3 * CHUNK), 1) & (CHUNK - 1)
    return jnp.where(ii >= ss if d == 0 else ii <= ss, 1.0, 0.0).astype(BF16)


def _rms(x, w):
    return x * lax.rsqrt(jnp.mean(x * x, axis=-1, keepdims=True) + EPS) * w


def _silu(x):
    return x * jax.nn.sigmoid(x)


def _softplus(x):
    return jnp.maximum(x, 0.0) + jnp.log(1.0 + jnp.exp(-jnp.abs(x)))


def _resident(shape):
    nd = len(shape)
    return pl.BlockSpec(shape, lambda *_: (0,) * nd, pipeline_mode=pl.Buffered(1))


def _mix_in(h, o_refs, gate_ref, on_ref, w_ref, scale):
    o = o_refs[0][...].astype(F32)
    for r in o_refs[1:]:
        o = o + r[...].astype(F32)
    on = on_ref[...]
    parts = []
    for hd in range(HEADS):
        sl = slice(hd * HEAD_DIM, (hd + 1) * HEAD_DIM)
        y = _rms(o[:, sl], on)
        if gate_ref is not None:
            y = y * _silu(gate_ref[:, sl].astype(F32))
        else:
            y = y * scale
        parts.append(y.astype(BF16))
    return h + _dot(jnp.concatenate(parts, axis=-1), w_ref[...])


def _ffn_kernel(x_ref, *refs, d_ff, n_split, final, n_o, gated, scale):
    x = x_ref[...]
    if n_o:
        o_refs, refs = refs[:n_o], refs[n_o:]
        gate_ref = refs[0] if gated else None
        on_ref, w_ref = refs[int(gated)], refs[int(gated) + 1]
        refs = refs[int(gated) + 2:]
        x = _mix_in(x, o_refs, gate_ref, on_ref, w_ref, scale)
    nw_ref, wup_ref, wdn_ref, fn_ref, o_ref = refs
    xn = _rms(x, nw_ref[...]).astype(BF16)
    step = d_ff // n_split
    y = None
    for c in range(n_split):
        g = _dot(xn, wup_ref[:, c * step:(c + 1) * step])
        u = _dot(xn, wup_ref[:, d_ff + c * step:d_ff + (c + 1) * step])
        a = (_silu(g) * u).astype(BF16)
        part = _dot(a, wdn_ref[c * step:(c + 1) * step, :])
        y = part if y is None else y + part
    h = x + 0.5 * y
    if final:
        h = _rms(h, fn_ref[...])
    o_ref[...] = h


def _ffn(h, nw, w_up, w_down, final_w, final, mix=None):
    t, d = h.shape
    d_ff = w_down.shape[0]
    n_split = max(n for n in range(1, FFN_MAX_SPLIT + 1) if d_ff % (n * LANES) == 0)
    tile = _wide_row_tile(t)
    row = pl.BlockSpec((tile, d), lambda i: (i, 0))
    in_specs, args = [row], [h]
    n_o, gated, scale = 0, False, 1.0
    if mix is not None:
        o_list, gate, gate_block, o_norm, w_out, scale = mix
        n_o, gated = len(o_list), gate is not None
        in_specs += [row] * n_o
        args += list(o_list)
        if gated:
            in_specs.append(pl.BlockSpec((tile, d), lambda i: (i, gate_block)))
            args.append(gate)
        in_specs += [_resident((1, HEAD_DIM)), _resident(w_out.shape)]
        args += [o_norm.reshape(1, HEAD_DIM), w_out]
    in_specs += [_resident((1, d)), _resident(w_up.shape), _resident(w_down.shape), _resident((1, d))]
    args += [nw.reshape(1, d), w_up, w_down, final_w.reshape(1, d)]
    return pl.pallas_call(
        functools.partial(_ffn_kernel, d_ff=d_ff, n_split=n_split, final=final,
                          n_o=n_o, gated=gated, scale=scale),
        grid=(pl.cdiv(t, tile),),
        in_specs=in_specs,
        out_specs=row,
        out_shape=jax.ShapeDtypeStruct((t, d), F32),
        compiler_params=_params("parallel"),
        name="ffn",
    )(*args)


def _proj_kernel(x_ref, nw_ref, *refs):
    n = len(refs) // 2
    xn = _rms(x_ref[...], nw_ref[...]).astype(BF16)
    for w_ref, o_ref in zip(refs[:n], refs[n:]):
        o_ref[...] = _dot(xn, w_ref[...])


def _norm_proj(h, nw, weights):
    t, d = h.shape
    row = pl.BlockSpec((ROW_TILE, d), lambda i: (i, 0))
    return pl.pallas_call(
        _proj_kernel,
        grid=(pl.cdiv(t, ROW_TILE),),
        in_specs=[row, _resident((1, d))] + [_resident(w.shape) for w in weights],
        out_specs=[pl.BlockSpec((ROW_TILE, w.shape[1]), lambda i: (i, 0)) for w in weights],
        out_shape=[jax.ShapeDtypeStruct((t, w.shape[1]), F32) for w in weights],
        compiler_params=_params("parallel"),
        name="norm_proj",
    )(h, nw.reshape(1, d), *weights)


CONV_HALO = 8


def _proj_conv_kernel(first_ref, xm_ref, xp_ref, xn_ref, nw_ref, wqkv_ref, wgate_ref, wba_ref, cw_ref,
                      qkv_ref, gate_ref, ba_ref, buf_ref, *, tile):
    i = pl.program_id(0)
    last = pl.num_programs(0) - 1
    width = HEADS * HEAD_DIM
    x_all = jnp.concatenate([jnp.where(i > 0, xp_ref[...], 0.0), xm_ref[...],
                             jnp.where(i < last, xn_ref[...], 0.0)], axis=0)
    xn_all = _rms(x_all, nw_ref[...]).astype(BF16)
    mid = slice(CONV_HALO, CONV_HALO + tile)
    half = (CONV_K - 1) // 2
    row = lax.broadcasted_iota(jnp.int32, (CHUNK, 1), 0)
    per_tile = tile // CHUNK

    def project(g):
        buf_ref[g] = _dot(xn_all, wqkv_ref[:, g * width:(g + 1) * width])

    def conv(g):
        cols = slice(g * width, (g + 1) * width)
        acc = None
        for k in range(CONV_K):
            lo = CONV_HALO - half + k
            term = cw_ref[k:k + 1, cols] * buf_ref[g, lo:lo + tile, :]
            acc = term if acc is None else acc + term
        for c in range(per_tile):
            n_dead = jnp.where(first_ref[i * per_tile + c] > 0, PAD, 0)
            y = jnp.where(row < n_dead, 0.0, acc[c * CHUNK:(c + 1) * CHUNK, :])
            y = _silu(y)
            for hd in range(HEADS):
                sl = slice(hd * HEAD_DIM, (hd + 1) * HEAD_DIM)
                blk = y[:, sl]
                if g < 2:
                    inv = lax.rsqrt(jnp.sum(blk * blk, axis=-1, keepdims=True) + EPS)
                    blk = blk * (inv * (HEAD_DIM ** -0.5) if g == 0 else inv)
                qkv_ref[c * CHUNK:(c + 1) * CHUNK, g * width + hd * HEAD_DIM:g * width + (hd + 1) * HEAD_DIM] = (
                    blk.astype(qkv_ref.dtype))

    project(0)
    project(1)
    conv(0)
    project(2)
    conv(1)
    gate_ref[...] = _dot(xn_all, wgate_ref[...])[mid].astype(gate_ref.dtype)
    ba_ref[...] = _dot(xn_all, wba_ref[...])[mid]
    conv(2)


def _norm_proj_conv(h, nw, w_in, conv_w, first):
    t, d = h.shape
    width = HEADS * HEAD_DIM
    n_ba = w_in.shape[1] - 4 * width
    tile = CONV_TILE if t % CONV_TILE == 0 else CHUNK
    n_halo = t // CONV_HALO
    tile_halo = tile // CONV_HALO

    def rows(n):
        return pl.BlockSpec((tile, n), lambda i, f: (i, 0))

    def const(shape):
        return pl.BlockSpec(shape, lambda i, f: (0,) * len(shape), pipeline_mode=pl.Buffered(1))

    grid_spec = pltpu.PrefetchScalarGridSpec(
        num_scalar_prefetch=1,
        grid=(t // tile,),
        in_specs=[
            rows(d),
            pl.BlockSpec((CONV_HALO, d), lambda i, f: (jnp.maximum(i * tile_halo - 1, 0), 0)),
            pl.BlockSpec((CONV_HALO, d), lambda i, f: (jnp.minimum((i + 1) * tile_halo, n_halo - 1), 0)),
            const((1, d)), const((d, 3 * width)), const((d, width)), const((d, n_ba)),
            const((CONV_K, 3 * width)),
        ],
        out_specs=[rows(3 * width), rows(width), rows(n_ba)],
        scratch_shapes=[pltpu.VMEM((3, tile + 2 * CONV_HALO, width), F32)],
    )
    return pl.pallas_call(
        functools.partial(_proj_conv_kernel, tile=tile),
        grid_spec=grid_spec,
        out_shape=[jax.ShapeDtypeStruct((t, 3 * width), BF16), jax.ShapeDtypeStruct((t, width), BF16),
                   jax.ShapeDtypeStruct((t, n_ba), F32)],
        compiler_params=_params("arbitrary"),
        name="norm_proj_conv",
    )(first, h, h, h, nw.reshape(1, d), w_in[:, :3 * width], w_in[:, 3 * width:4 * width],
      w_in[:, 4 * width:], conv_w)


def _scan_chunk(d, s, nc):
    return s if d == 0 else nc - 1 - s


def _order_masks(d):
    ii = lax.broadcasted_iota(jnp.int32, (CHUNK, CHUNK), 0)
    jj = lax.broadcasted_iota(jnp.int32, (CHUNK, CHUNK), 1)
    return (ii >= jj, ii > jj) if d == 0 else (ii <= jj, ii < jj)


def _delta_kernel(first_ref, last_ref, qf_ref, kf_ref, vf_ref, qb_ref, kb_ref, vb_ref,
                  bacf_ref, bacb_ref, barf_ref, barb_ref,
                  alog_r_ref, dtb_r_ref, alog_c_ref, dtb_c_ref, of_ref, ob_ref, s_ref, *, cps):
    s = pl.program_id(0)
    nb = pl.num_programs(0)
    block = (s, nb - 1 - s)

    @pl.when(s == 0)
    def _():
        s_ref[...] = jnp.zeros_like(s_ref)

    ii = lax.broadcasted_iota(jnp.int32, (CHUNK, CHUNK), 0)
    jj = lax.broadcasted_iota(jnp.int32, (CHUNK, CHUNK), 1)
    eye = jnp.where(ii == jj, 1.0, 0.0)
    n_lvl = int(math.log2(CHUNK))
    lvl = [((ii >> (b + 1)) == (jj >> (b + 1))) & ((ii >> b) != (jj >> b)) for b in range(n_lvl)]

    q_refs, k_refs, v_refs = (qf_ref, qb_ref), (kf_ref, kb_ref), (vf_ref, vb_ref)
    bac_refs, bar_refs, o_refs = (bacf_ref, bacb_ref), (barf_ref, barb_ref), (of_ref, ob_ref)
    incl, strict, m_incl3, m_incl = [], [], [], []
    for d in range(2):
        inc, strc = _order_masks(d)
        incl.append(inc)
        strict.append(strc)
        m_incl3.append(_order_mask3(d))
        m_incl.append(jnp.where(inc, 1.0, 0.0).astype(BF16))
    rows = [slice(c * CHUNK, (c + 1) * CHUNK) for c in range(cps)]
    beta_c, gc_c, gc_r, gt_c = {}, {}, {}, {}
    for d in range(2):
        for c in range(cps):
            bac = bac_refs[d][rows[c], :]
            bar = bar_refs[d][c]
            la_c = -jnp.exp(alog_r_ref[d]) * _softplus(bac[:, HEADS:] + dtb_r_ref[d])
            la_r = -jnp.exp(alog_c_ref[d]) * _softplus(bar[HEADS:, :] + dtb_c_ref[d])
            beta_c[d, c] = jax.nn.sigmoid(bac[:, :HEADS])
            gc_c[d, c] = _mask_dot(m_incl3[d], la_c)
            gc_r[d, c] = _dot_mask_nt(la_r, m_incl[d])
            gt_c[d, c] = jnp.sum(la_c, axis=0, keepdims=True)

    units = [(d, c, hd) for d in range(2) for c in range(cps) for hd in range(HEADS)]
    n = range(len(units))
    sls = [slice(hd * HEAD_DIM, (hd + 1) * HEAD_DIM) for _, _, hd in units]
    qs = [q_refs[d][rows[c], sls[x]] for x, (d, c, _) in enumerate(units)]
    ks = [k_refs[d][rows[c], sls[x]] for x, (d, c, _) in enumerate(units)]
    vs = [v_refs[d][rows[c], sls[x]] for x, (d, c, _) in enumerate(units)]
    g_col = [gc_c[d, c][:, hd:hd + 1] for d, c, hd in units]
    g_tot = [gt_c[d, c][:, hd:hd + 1] for d, c, hd in units]
    b_col = [beta_c[d, c][:, hd:hd + 1] for d, c, hd in units]
    decay = [jnp.exp(jnp.where(incl[d], g_col[x] - gc_r[d, c][hd:hd + 1, :], -jnp.inf))
             for x, (d, c, hd) in enumerate(units)]
    kb = [ks[x] * b_col[x] for x in n]
    k16 = [x.astype(BF16) for x in ks]
    a_mat = [jnp.where(strict[units[x][0]], _dot_nt(kb[x].astype(BF16), k16[x]) * decay[x], 0.0) for x in n]
    t_inv = [eye - jnp.where(lvl[0], a, 0.0) for a in a_mat]
    for b in range(1, n_lvl):
        t16 = [t.astype(BF16) for t in t_inv]
        a_off = [jnp.where(lvl[b], a, 0.0).astype(BF16) for a in a_mat]
        x16 = [_dot(a_off[x], t16[x]).astype(BF16) for x in n]
        t_inv = [t_inv[x] - _dot(t16[x], x16[x]) for x in n]
    t16 = [t.astype(BF16) for t in t_inv]
    e_col = [jnp.exp(g) for g in g_col]
    uw = [_dot(t16[x], jnp.concatenate([(vs[x] * b_col[x]).astype(BF16),
                                        (kb[x] * e_col[x]).astype(BF16)], axis=1)) for x in n]
    u = [y[:, :HEAD_DIM] for y in uw]
    w16 = [y[:, HEAD_DIM:].astype(BF16) for y in uw]
    qk16 = [(_dot_nt(qs[x].astype(BF16), k16[x]) * decay[x]).astype(BF16) for x in n]
    qd16 = [(qs[x] * e_col[x]).astype(BF16) for x in n]
    kd16 = [(ks[x] * jnp.exp(g_tot[x] - g_col[x])).astype(BF16) for x in n]
    a_last = [jnp.exp(g) for g in g_tot]

    index = {unit: x for x, unit in enumerate(units)}
    state = {(d, hd): s_ref[d, hd] for d in range(2) for hd in range(HEADS)}
    for t in range(cps):
        slot = (t, cps - 1 - t)
        fresh = (first_ref[block[0] * cps + slot[0]] > 0, last_ref[block[1] * cps + slot[1]] > 0)
        cur = [(d, slot[d], hd) for d in range(2) for hd in range(HEADS)]
        idx = [index[k] for k in cur]
        st = [jnp.where(fresh[d], 0.0, state[d, hd]) for d, _, hd in cur]
        st16 = [x.astype(BF16) for x in st]
        ws = [_dot(jnp.concatenate([w16[idx[y]], qd16[idx[y]]], axis=0), st16[y]) for y in range(len(cur))]
        vn16 = [(u[idx[y]] - ws[y][:CHUNK]).astype(BF16) for y in range(len(cur))]
        for y, (d, c, _) in enumerate(cur):
            o_refs[d][rows[c], sls[idx[y]]] = (ws[y][CHUNK:]
                                               + _dot(qk16[idx[y]], vn16[y])).astype(o_refs[d].dtype)
        for y, (d, _, hd) in enumerate(cur):
            state[d, hd] = st[y] * a_last[idx[y]] + _dot_tn(kd16[idx[y]], vn16[y])
    for (d, hd), val in state.items():
        s_ref[d, hd] = val


def _delta_rule(qkv, ba_c, ba_r, a_log, dt_bias, first, last):
    t = qkv.shape[0]
    nc = t // CHUNK
    cps = SCAN_CHUNKS_PER_STEP if nc % SCAN_CHUNKS_PER_STEP == 0 else 1
    nb = nc // cps
    width = HEADS * HEAD_DIM

    def col(j, d):
        return pl.BlockSpec((cps * CHUNK, width), lambda s, f, l: (_scan_chunk(d, s, nb), j))

    def bac(d):
        return pl.BlockSpec((None, cps * CHUNK, 2 * HEADS), lambda s, f, l: (d, _scan_chunk(d, s, nb), 0))

    def bar(d):
        return pl.BlockSpec((None, cps, 2 * HEADS, CHUNK), lambda s, f, l: (d, _scan_chunk(d, s, nb), 0, 0))

    par_r = pl.BlockSpec((2, 1, HEADS), lambda s, f, l: (0, 0, 0))
    par_c = pl.BlockSpec((2, HEADS, 1), lambda s, f, l: (0, 0, 0))
    grid_spec = pltpu.PrefetchScalarGridSpec(
        num_scalar_prefetch=2,
        grid=(nb,),
        in_specs=[col(0, 0), col(1, 0), col(2, 0), col(0, 1), col(1, 1), col(2, 1),
                  bac(0), bac(1), bar(0), bar(1), par_r, par_r, par_c, par_c],
        out_specs=[col(0, 0), col(0, 1)],
        scratch_shapes=[pltpu.VMEM((2, HEADS, HEAD_DIM, HEAD_DIM), F32)],
    )
    return pl.pallas_call(
        functools.partial(_delta_kernel, cps=cps),
        grid_spec=grid_spec,
        out_shape=[jax.ShapeDtypeStruct((t, width), BF16)] * 2,
        compiler_params=_params("arbitrary"),
        name="delta_rule",
    )(first, last, qkv, qkv, qkv, qkv, qkv, qkv, ba_c, ba_c, ba_r, ba_r,
      a_log.reshape(2, 1, HEADS), dt_bias.reshape(2, 1, HEADS),
      a_log.reshape(2, HEADS, 1), dt_bias.reshape(2, HEADS, 1))


def _gated_deltanet(h, nw, w_in, conv_w, a_log, dt_bias, o_norm, w_out, first, last):
    t = h.shape[0]
    width = HEADS * HEAD_DIM
    qkv, gate, ba = _norm_proj_conv(h, nw, w_in, conv_w, first)
    ba = ba.reshape(t, 2, 2, HEADS)
    ba_c = jnp.transpose(ba, (2, 0, 1, 3)).reshape(2, t, 2 * HEADS)
    ba_r = jnp.transpose(ba_c.reshape(2, t // CHUNK, CHUNK, 2 * HEADS), (0, 1, 3, 2))
    o_f, o_b = _delta_rule(qkv, ba_c, ba_r, a_log, dt_bias, first, last)
    return ([o_f, o_b], gate, 0, o_norm, w_out, 1.0)


def _gla_level_matrices():
    ii = np.arange(CHUNK)[:, None]
    jj = np.arange(CHUNK)[None, :]
    out = []
    for d in range(2):
        mats = [(ii >= jj) if d == 0 else (ii <= jj)]
        for b in range(int(math.log2(CHUNK))):
            size = 1 << b
            mid = ((ii >> (b + 1)) << (b + 1)) + size
            upper = (ii & size) != 0
            if d == 0:
                sel = (upper & (jj >= mid) & (jj <= ii)) | (~upper & (jj > ii) & (jj < mid))
            else:
                sel = (upper & (jj >= mid) & (jj < ii)) | (~upper & (jj >= ii) & (jj < mid))
            mats.append(sel)
        out.append(np.tile(np.concatenate(mats, axis=0), (1, 3)))
    return np.stack(out).astype(np.float32)


def _gla_kernel(first_ref, last_ref, qf_ref, vf_ref, ff_ref, qb_ref, vb_ref, fb_ref, lb_ref, lvl_ref,
                of_ref, ob_ref, s_ref, *, layer_idx, cps):
    s = pl.program_id(0)
    nb = pl.num_programs(0)
    block = (s, nb - 1 - s)

    @pl.when(s == 0)
    def _():
        s_ref[...] = jnp.zeros_like(s_ref)

    ii = lax.broadcasted_iota(jnp.int32, (CHUNK, CHUNK), 0)
    jj = lax.broadcasted_iota(jnp.int32, (CHUNK, CHUNK), 1)
    n_lvl = int(math.log2(CHUNK))

    lbl = lb_ref[...]
    lbw = jnp.exp(lbl - jnp.max(lbl, axis=0, keepdims=True))
    lbw = lbw / jnp.sum(lbw, axis=0, keepdims=True)
    lb = jnp.zeros_like(lbw[0:1, :])
    for r in range(1, layer_idx + 1):
        lb = lb + lbw[r:r + 1, :]

    q_refs, v_refs, f_refs, o_refs = (qf_ref, qb_ref), (vf_ref, vb_ref), (ff_ref, fb_ref), (of_ref, ob_ref)
    rows = [slice(c * CHUNK, (c + 1) * CHUNK) for c in range(cps)]
    pair_masks = []
    for d in range(2):
        _, strict = _order_masks(d)
        pair_masks.append([((ii >> (b + 1)) == (jj >> (b + 1))) & ((ii >> b) != (jj >> b)) & strict
                           for b in range(n_lvl)])
    q, kk, x_all, q_dec, k_dec, f_last = {}, {}, {}, {}, {}, {}
    for d in range(2):
        for c in range(cps):
            f = lb + (1.0 - lb) * jax.nn.sigmoid(f_refs[d][rows[c], :])
            logf = jnp.maximum(jnp.log(f), LOG_GATE_FLOOR)
            e_all = _mask_dot(lvl_ref[d], logf)
            tot = jnp.sum(logf, axis=0, keepdims=True)
            xa = jnp.exp(e_all)
            q[d, c] = _silu(q_refs[d][rows[c], :]) * (HEAD_DIM ** -0.5)
            kk[d, c] = 1.0 - f
            x_all[d, c] = xa.astype(BF16)
            q_dec[d, c] = q[d, c] * xa[:CHUNK, :]
            k_dec[d, c] = (1.0 - f) * jnp.exp(tot - e_all[:CHUNK, :])
            f_last[d, c] = jnp.exp(tot)

    units = [(d, c, hd) for d in range(2) for c in range(cps) for hd in range(HEADS)]
    n = range(len(units))
    sls = [slice(hd * HEAD_DIM, (hd + 1) * HEAD_DIM) for _, _, hd in units]
    q16 = [q[d, c][:, sls[x]].astype(BF16) for x, (d, c, _) in enumerate(units)]
    k16 = [kk[d, c][:, sls[x]].astype(BF16) for x, (d, c, _) in enumerate(units)]
    attn = [jnp.where(ii == jj, _dot_nt(q16[x], k16[x]), 0.0) for x in n]
    for b in range(n_lvl):
        xb = [x_all[d, c][(b + 1) * CHUNK:(b + 2) * CHUNK, sls[x]] for x, (d, c, _) in enumerate(units)]
        part = [_dot_nt(q16[x] * xb[x], k16[x] * xb[x]) for x in n]
        attn = [jnp.where(pair_masks[units[x][0]][b], part[x], attn[x]) for x in n]
    attn16 = [a.astype(BF16) for a in attn]
    v16 = [v_refs[d][rows[c], sls[x]].astype(BF16) for x, (d, c, _) in enumerate(units)]
    qd16 = [q_dec[d, c][:, sls[x]].astype(BF16) for x, (d, c, _) in enumerate(units)]
    kd16 = [k_dec[d, c][:, sls[x]].astype(BF16) for x, (d, c, _) in enumerate(units)]

    index = {unit: x for x, unit in enumerate(units)}
    state = {(d, hd): s_ref[d, hd] for d in range(2) for hd in range(HEADS)}
    for t in range(cps):
        slot = (t, cps - 1 - t)
        fresh = (first_ref[block[0] * cps + slot[0]] > 0, last_ref[block[1] * cps + slot[1]] > 0)
        cur = [(d, slot[d], hd) for d in range(2) for hd in range(HEADS)]
        idx = [index[k] for k in cur]
        st = [jnp.where(fresh[d], 0.0, state[d, hd]) for d, _, hd in cur]
        for y, (d, c, _) in enumerate(cur):
            o_refs[d][rows[c], sls[idx[y]]] = (_dot_nt(qd16[idx[y]], st[y].astype(BF16))
                                               + _dot(attn16[idx[y]], v16[idx[y]])).astype(o_refs[d].dtype)
        for y, (d, c, hd) in enumerate(cur):
            state[d, hd] = st[y] * f_last[d, c][:, sls[idx[y]]] + _dot_tn(v16[idx[y]], kd16[idx[y]])
    for (d, hd), val in state.items():
        s_ref[d, hd] = val


def _gla(proj, lb_logits, layer_idx, first, last):
    t = proj.shape[0]
    nc = t // CHUNK
    cps = SCAN_CHUNKS_PER_STEP if nc % SCAN_CHUNKS_PER_STEP == 0 else 1
    nb = nc // cps
    width = HEADS * HEAD_DIM

    lvl = jnp.asarray(_gla_level_matrices(), BF16)

    def col(j, d):
        return pl.BlockSpec((cps * CHUNK, width), lambda s, f, l: (_scan_chunk(d, s, nb), j))

    grid_spec = pltpu.PrefetchScalarGridSpec(
        num_scalar_prefetch=2,
        grid=(nb,),
        in_specs=[col(0, 0), col(1, 0), col(3, 0), col(0, 1), col(1, 1), col(4, 1),
                  pl.BlockSpec(lb_logits.shape, lambda s, f, l: (0, 0)),
                  pl.BlockSpec(lvl.shape, lambda s, f, l: (0, 0, 0))],
        out_specs=[col(0, 0), col(0, 1)],
        scratch_shapes=[pltpu.VMEM((2, HEADS, HEAD_DIM, HEAD_DIM), F32)],
    )
    return pl.pallas_call(
        functools.partial(_gla_kernel, layer_idx=layer_idx, cps=cps),
        grid_spec=grid_spec,
        out_shape=[jax.ShapeDtypeStruct((t, width), BF16)] * 2,
        compiler_params=_params("arbitrary"),
        name="gla",
    )(first, last, proj, proj, proj, proj, proj, proj, lb_logits, lvl)


def _hgrn2(h, nw, w_in, lb_logits, layer_idx, o_norm, w_out, first, last):
    (proj,) = _norm_proj(h, nw, [w_in])
    o_f, o_b = _gla(proj, lb_logits, layer_idx, first, last)
    return ([o_f, o_b], proj, 2, o_norm, w_out, 1.0)


def _proj_rope_kernel(x_ref, nw_ref, w_ref, cos_ref, s1_ref, s2_ref, o_ref, *, n_rot_tiles, n_q_tiles):
    xn = _rms(x_ref[...], nw_ref[...]).astype(BF16)
    cos = cos_ref[...]
    s1 = s1_ref[...]
    s2 = s2_ref[...]
    half = ATT_ROT // 2
    for c2 in range(0, n_rot_tiles, 2):
        wide = _dot(xn, w_ref[:, c2 * LANES:(c2 + 2) * LANES])
        for c in (c2, c2 + 1):
            sl = slice(c * LANES, (c + 1) * LANES)
            x = wide[:, (c - c2) * LANES:(c - c2 + 1) * LANES]
            y = x * cos + pltpu.roll(x, LANES - half, 1) * s1 + pltpu.roll(x, half, 1) * s2
            if c < n_q_tiles:
                y = y * (ATT_HD ** -0.5 * LOG2E)
            o_ref[:, sl] = y.astype(BF16)
    rest = slice(n_rot_tiles * LANES, w_ref.shape[1])
    o_ref[:, rest] = _dot(xn, w_ref[:, rest]).astype(BF16)


def _norm_proj_rope(h, nw, w_in, cos_t, s1_t, s2_t):
    t, d = h.shape
    n = w_in.shape[1]
    qk_w = HEADS * ATT_HD
    row = pl.BlockSpec((ROW_TILE, d), lambda i: (i, 0))
    tab = pl.BlockSpec((ROW_TILE, LANES), lambda i: (i, 0))
    return pl.pallas_call(
        functools.partial(_proj_rope_kernel, n_rot_tiles=4 * qk_w // LANES, n_q_tiles=2 * qk_w // LANES),
        grid=(pl.cdiv(t, ROW_TILE),),
        in_specs=[row, _resident((1, d)), _resident(w_in.shape), tab, tab, tab],
        out_specs=pl.BlockSpec((ROW_TILE, n), lambda i: (i, 0)),
        out_shape=jax.ShapeDtypeStruct((t, n), BF16),
        compiler_params=_params("parallel"),
        name="norm_proj_rope",
    )(h, nw.reshape(1, d), w_in, cos_t, s1_t, s2_t)


def _key_tiles(seq_len):
    n = max(1, seq_len // ATT_TK)
    return [ATT_TK] * (n - 1) + [seq_len - (n - 1) * ATT_TK]


def _attn_kernel(q1_ref, q2_ref, k1_ref, k2_ref, v_ref, lam_ref, o_ref, vt_ref, *scratch,
                 seq_len, lam_init):
    i = pl.program_id(2)
    sizes = _key_tiles(seq_len)
    n_tiles = len(sizes)
    acc_refs, sc_refs = scratch[:4], scratch[4:]

    @pl.when(i == 0)
    def _():
        def body(j, carry):
            blk = v_ref[pl.ds(pl.multiple_of(j * ATT_TK, ATT_TK), ATT_TK), :]
            vt_ref[j, :, :ATT_TK] = blk.astype(F32).T.astype(BF16)
            return carry
        if n_tiles > 1:
            lax.fori_loop(0, n_tiles - 1, body, 0)
        blk = v_ref[(n_tiles - 1) * ATT_TK:seq_len, :]
        vt_ref[n_tiles - 1, :, :sizes[-1]] = blk.astype(F32).T.astype(BF16)

    lane = lax.broadcasted_iota(jnp.int32, (1, LANES), 1)
    q_refs = (q1_ref, q2_ref)
    k_refs = (k1_ref, k2_ref)
    qm = []
    for hh in range(2):
        keep = (lane >= hh * ATT_HD) & (lane < (hh + 1) * ATT_HD)
        qm.append([jnp.where(keep, q_refs[m][...], jnp.zeros((), BF16)) for m in range(2)])

    for a in acc_refs:
        a[...] = jnp.zeros_like(a)
    units = [(hh, m) for hh in range(2) for m in range(2)]

    def score(u, j, size, masked):
        hh, m = units[u]
        start = j * ATT_TK
        if not isinstance(j, int):
            start = pl.multiple_of(start, ATT_TK)
        sc = _dot_nt(k_refs[m][pl.ds(start, size), :], qm[hh][m])
        if masked:
            rowid = lax.broadcasted_iota(jnp.int32, (size, 1), 0)
            sc = jnp.where(rowid >= PAD, sc, -jnp.inf)
        return sc

    def consume(u, sc, j, size, carry):
        hh = units[u][0]
        vt = vt_ref[j, hh * HEAD_DIM:(hh + 1) * HEAD_DIM, :size]
        m_old, l_old = carry[u], carry[4 + u]
        m_new = jnp.maximum(m_old, jnp.max(sc, axis=0, keepdims=True))
        alpha = jnp.exp2(m_old - m_new)
        p = jnp.exp2(sc - m_new)
        l_new = alpha * l_old + jnp.sum(p, axis=0, keepdims=True)
        acc_refs[u][...] = alpha * acc_refs[u][...] + _dot(vt, p.astype(BF16))
        return m_new, l_new

    bufs = (sc_refs[:4], sc_refs[4:])

    def step(j, size, carry, cur, nxt=None, size_n=None):
        def issue(u):
            nxt[u][:size_n, :] = score(u, j + 1, size_n, False)
        if nxt is not None:
            issue(0)
            issue(1)
        stats = []
        for u in range(4):
            stats.append(consume(u, cur[u][:size, :], j, size, carry))
            if nxt is not None and u + 2 < 4:
                issue(u + 2)
        return tuple(s[0] for s in stats) + tuple(s[1] for s in stats)

    carry = (jnp.full((1, ATT_TQ), -jnp.inf, F32),) * 4 + (jnp.zeros((1, ATT_TQ), F32),) * 4
    for u in range(4):
        bufs[0][u][:sizes[0], :] = score(u, 0, sizes[0], True)
    n_pairs = max(0, (n_tiles - 2) // 2)
    if n_pairs > 0:
        def body(i, c):
            c = step(2 * i, ATT_TK, c, bufs[0], bufs[1], ATT_TK)
            return step(2 * i + 1, ATT_TK, c, bufs[1], bufs[0], ATT_TK)
        carry = lax.fori_loop(0, n_pairs, body, carry)
    for t in range(2 * n_pairs, n_tiles):
        if t + 1 < n_tiles:
            carry = step(t, sizes[t], carry, bufs[t % 2], bufs[(t + 1) % 2], sizes[t + 1])
        else:
            carry = step(t, sizes[t], carry, bufs[t % 2])
    l_fin = carry[4:]

    lv = lam_ref[...]
    lam = (jnp.exp(jnp.sum(lv[0:1] * lv[1:2], axis=-1, keepdims=True))
           - jnp.exp(jnp.sum(lv[2:3] * lv[3:4], axis=-1, keepdims=True)) + lam_init)
    row = i * ATT_TQ + lax.broadcasted_iota(jnp.int32, (ATT_TQ, 1), 0)
    for hh in range(2):
        o_t = (acc_refs[2 * hh][...] / l_fin[2 * hh]
               - lam * (acc_refs[2 * hh + 1][...] / l_fin[2 * hh + 1]))
        o_ref[:, hh * HEAD_DIM:(hh + 1) * HEAD_DIM] = jnp.where(row >= PAD, o_t.T, 0.0).astype(o_ref.dtype)


def _attention(qkv, lam_vec, lam_init):
    b, seq_len, _ = qkv.shape
    width = HEADS * HEAD_DIM
    n_pair = HEADS // 2
    pair_w = 2 * ATT_HD
    nq = pl.cdiv(seq_len, ATT_TQ)
    sizes = _key_tiles(seq_len)

    def qspec(off):
        return pl.BlockSpec((None, ATT_TQ, pair_w), lambda bb, p, i: (bb, i, off + p))

    def kspec(off):
        return pl.BlockSpec((None, seq_len, pair_w), lambda bb, p, i: (bb, 0, off + p))

    return pl.pallas_call(
        functools.partial(_attn_kernel, seq_len=seq_len, lam_init=lam_init),
        grid=(b, n_pair, nq),
        in_specs=[
            qspec(0), qspec(n_pair), kspec(2 * n_pair), kspec(3 * n_pair),
            pl.BlockSpec((None, seq_len, 2 * HEAD_DIM), lambda bb, p, i: (bb, 0, 2 * n_pair + p)),
            pl.BlockSpec(lam_vec.shape, lambda bb, p, i: (0, 0)),
        ],
        out_specs=pl.BlockSpec((None, ATT_TQ, 2 * HEAD_DIM), lambda bb, p, i: (bb, i, p)),
        out_shape=jax.ShapeDtypeStruct((b, seq_len, width), BF16),
        scratch_shapes=[pltpu.VMEM((len(sizes), 2 * HEAD_DIM, max(sizes)), BF16)]
        + [pltpu.VMEM((HEAD_DIM, ATT_TQ), F32)] * 4
        + [pltpu.VMEM((max(sizes), ATT_TQ), F32)] * 8,
        compiler_params=_params("arbitrary", "arbitrary", "arbitrary"),
        name="diff_attention",
    )(qkv, qkv, qkv, qkv, qkv, lam_vec)


def _rope_tables(pos):
    half = ATT_ROT // 2
    inv_freq = jnp.exp(-math.log(ROPE_THETA) * jnp.arange(half, dtype=F32) / half)
    ang = pos.astype(F32)[:, None] * inv_freq
    cos, sin = jnp.cos(ang), jnp.sin(ang)
    n = pos.shape[0]
    rest = ATT_HD - ATT_ROT
    cos_t = jnp.concatenate([cos, cos, jnp.ones((n, rest), F32)], axis=1)
    s1_t = jnp.concatenate([-sin, jnp.zeros((n, ATT_HD - half), F32)], axis=1)
    s2_t = jnp.concatenate([jnp.zeros((n, half), F32), sin, jnp.zeros((n, rest), F32)], axis=1)
    rep = LANES // ATT_HD
    return tuple(jnp.tile(x, (1, rep)) for x in (cos_t, s1_t, s2_t))


def _diff_attention(h, nw, w_in, lam_vec, sub_norm, w_out, layer_idx, groups, pos):
    roped = _norm_proj_rope(h, nw, w_in, *_rope_tables(pos))
    lam_init = 0.8 - 0.6 * math.exp(-0.3 * layer_idx)
    outs = []
    start = 0
    for b, seq_len in groups:
        rows = b * seq_len
        part = roped[start:start + rows].reshape(b, seq_len, roped.shape[1])
        outs.append(_attention(part, lam_vec, lam_init).reshape(rows, -1))
        start += rows
    o = jnp.concatenate(outs, axis=0)
    return ([o], None, 0, sub_norm, w_out, 1.0 - lam_init)


def kernel(x_prompt, x_sample, meta_tokens, norm_w, ffn_w_up, ffn_w_down,
           a_w_in, a_conv_w, a_log, a_dt_bias, a_o_norm, a_w_out,
           b_w_in, b_lambda, b_sub_norm, b_w_out,
           c_w_in, c_lb_logits, c_o_norm, c_w_out, final_norm):
    d = x_prompt.shape[-1]
    depth = norm_w.shape[0]
    xs = (x_prompt, x_sample)
    groups = [(x.shape[0], x.shape[1] + CHUNK) for x in xs]

    lead = jnp.concatenate([jnp.zeros((PAD, d), F32), meta_tokens.astype(F32)], axis=0)
    parts = []
    for x in xs:
        for b in range(x.shape[0]):
            parts += [lead, x[b].astype(F32)]
    h = jnp.concatenate(parts, axis=0)
    t = h.shape[0]

    first = np.zeros((t // CHUNK,), np.int32)
    last = np.zeros((t // CHUNK,), np.int32)
    pos = np.zeros((t,), np.int32)
    start = 0
    for b, seq_len in groups:
        for _ in range(b):
            first[start // CHUNK] = 1
            last[(start + seq_len) // CHUNK - 1] = 1
            pos[start:start + seq_len] = np.maximum(np.arange(seq_len) - PAD, 0)
            start += seq_len
    first, last, pos = jnp.asarray(first), jnp.asarray(last), jnp.asarray(pos)

    w_up16 = ffn_w_up.astype(BF16)
    w_dn16 = ffn_w_down.astype(BF16)
    for i in range(depth):
        kind, j = i % N_MIXERS, i // N_MIXERS
        h = _ffn(h, norm_w[i, 0], w_up16[i, 0], w_dn16[i, 0], final_norm, False)
        if kind == 0:
            mix = _gated_deltanet(h, norm_w[i, 1], a_w_in[j].astype(BF16), a_conv_w[j], a_log[j],
                                  a_dt_bias[j], a_o_norm[j], a_w_out[j].astype(BF16), first, last)
        elif kind == 1:
            mix = _diff_attention(h, norm_w[i, 1], b_w_in[j].astype(BF16), b_lambda[j], b_sub_norm[j],
                                  b_w_out[j].astype(BF16), i, groups, pos)
        else:
            mix = _hgrn2(h, norm_w[i, 1], c_w_in[j].astype(BF16), c_lb_logits, i, c_o_norm[j],
                         c_w_out[j].astype(BF16), first, last)
        h = _ffn(h, norm_w[i, 2], w_up16[i, 1], w_dn16[i, 1], final_norm, i == depth - 1, mix=mix)

    h3 = h.reshape(t // CHUNK, CHUNK, d)
    outs = []
    start = 0
    for (b, seq_len), x in zip(groups, xs):
        per_seq = seq_len // CHUNK
        pieces = [h3[start + s * per_seq + 1:start + (s + 1) * per_seq] for s in range(b)]
        outs.append(jnp.concatenate(pieces, axis=0).reshape(b, seq_len - CHUNK, d).astype(x.dtype))
        start += b * per_seq
    return tuple(outs)
```

```python
import functools
import math

import numpy as np
import jax
import jax.numpy as jnp
from jax import lax
from jax.experimental import pallas as pl
from jax.experimental.pallas import tpu as pltpu

F32 = jnp.float32
BF16 = jnp.bfloat16

EPS = 1e-6
CHUNK = 64
N_META = 16
PAD = CHUNK - N_META
N_MIXERS = 3
CONV_K = 5
ROPE_THETA = 500000.0
HEADS = 8
HEAD_DIM = 128
ATT_HD = 64
ATT_ROT = ATT_HD // 4
LANES = 128
V7X_VMEM_LIMIT_BYTES = 56 * 1024 * 1024

ROW_TILE = 512
WIDE_ROW_TILE = 896
CONV_TILE = 448
ATT_TQ = 256
ATT_TK = 512
LOG2E = 1.4426950408889634
LOG_GATE_FLOOR = -1e30
FFN_MAX_SPLIT = 11
SCAN_CHUNKS_PER_STEP = 2


def _wide_row_tile(t):
    return WIDE_ROW_TILE if t % WIDE_ROW_TILE == 0 else ROW_TILE


def _params(*sem):
    return pltpu.CompilerParams(dimension_semantics=sem, vmem_limit_bytes=V7X_VMEM_LIMIT_BYTES)


def _dot(a, b):
    return jnp.dot(a, b, preferred_element_type=F32)


def _dot_nt(a, b):
    return lax.dot_general(a, b, (((1,), (1,)), ((), ())), preferred_element_type=F32)


def _dot_tn(a, b):
    return lax.dot_general(a, b, (((0,), (0,)), ((), ())), preferred_element_type=F32)


def _split3(x):
    hi = x.astype(BF16)
    r = x - hi.astype(F32)
    mid = r.astype(BF16)
    lo = (r - mid.astype(F32)).astype(BF16)
    return hi, mid, lo


def _mask_dot(mask3, x):
    return _dot(mask3, jnp.concatenate(_split3(x), axis=0))


def _dot_mask_nt(x, mask):
    hi, mid, lo = _split3(x)
    return _dot_nt(hi, mask) + _dot_nt(mid, mask) + _dot_nt(lo, mask)


def _order_mask3(d):
    ii = lax.broadcasted_iota(jnp.int32, (CHUNK, 3 * CHUNK), 0)
    ss = lax.broadcasted_iota(jnp.int32, (CHUNK, 3 * CHUNK), 1) & (CHUNK - 1)
    return jnp.where(ii >= ss if d == 0 else ii <= ss, 1.0, 0.0).astype(BF16)


def _rms(x, w):
    return x * lax.rsqrt(jnp.mean(x * x, axis=-1, keepdims=True) + EPS) * w


def _silu(x):
    return x * jax.nn.sigmoid(x)


def _softplus(x):
    return jnp.maximum(x, 0.0) + jnp.log(1.0 + jnp.exp(-jnp.abs(x)))


def _resident(shape):
    nd = len(shape)
    return pl.BlockSpec(shape, lambda *_: (0,) * nd, pipeline_mode=pl.Buffered(1))


def _mix_in(h, o_refs, gate_ref, on_ref, w_ref, scale):
    o = o_refs[0][...].astype(F32)
    for r in o_refs[1:]:
        o = o + r[...].astype(F32)
    on = on_ref[...]
    parts = []
    for hd in range(HEADS):
        sl = slice(hd * HEAD_DIM, (hd + 1) * HEAD_DIM)
        y = _rms(o[:, sl], on)
        if gate_ref is not None:
            y = y * _silu(gate_ref[:, sl].astype(F32))
        else:
            y = y * scale
        parts.append(y.astype(BF16))
    return h + _dot(jnp.concatenate(parts, axis=-1), w_ref[...])


def _ffn_kernel(x_ref, *refs, d_ff, n_split, final, n_o, gated, scale):
    x = x_ref[...]
    if n_o:
        o_refs, refs = refs[:n_o], refs[n_o:]
        gate_ref = refs[0] if gated else None
        on_ref, w_ref = refs[int(gated)], refs[int(gated) + 1]
        refs = refs[int(gated) + 2:]
        x = _mix_in(x, o_refs, gate_ref, on_ref, w_ref, scale)
    nw_ref, wup_ref, wdn_ref, fn_ref, o_ref = refs
    xn = _rms(x, nw_ref[...]).astype(BF16)
    step = d_ff // n_split
    y = None
    for c in range(n_split):
        g = _dot(xn, wup_ref[:, c * step:(c + 1) * step])
        u = _dot(xn, wup_ref[:, d_ff + c * step:d_ff + (c + 1) * step])
        a = (_silu(g) * u).astype(BF16)
        part = _dot(a, wdn_ref[c * step:(c + 1) * step, :])
        y = part if y is None else y + part
    h = x + 0.5 * y
    if final:
        h = _rms(h, fn_ref[...])
    o_ref[...] = h


def _ffn(h, nw, w_up, w_down, final_w, final, mix=None):
    t, d = h.shape
    d_ff = w_down.shape[0]
    n_split = max(n for n in range(1, FFN_MAX_SPLIT + 1) if d_ff % (n * LANES) == 0)
    tile = _wide_row_tile(t)
    row = pl.BlockSpec((tile, d), lambda i: (i, 0))
    in_specs, args = [row], [h]
    n_o, gated, scale = 0, False, 1.0
    if mix is not None:
        o_list, gate, gate_block, o_norm, w_out, scale = mix
        n_o, gated = len(o_list), gate is not None
        in_specs += [row] * n_o
        args += list(o_list)
        if gated:
            in_specs.append(pl.BlockSpec((tile, d), lambda i: (i, gate_block)))
            args.append(gate)
        in_specs += [_resident((1, HEAD_DIM)), _resident(w_out.shape)]
        args += [o_norm.reshape(1, HEAD_DIM), w_out]
    in_specs += [_resident((1, d)), _resident(w_up.shape), _resident(w_down.shape), _resident((1, d))]
    args += [nw.reshape(1, d), w_up, w_down, final_w.reshape(1, d)]
    return pl.pallas_call(
        functools.partial(_ffn_kernel, d_ff=d_ff, n_split=n_split, final=final,
                          n_o=n_o, gated=gated, scale=scale),
        grid=(pl.cdiv(t, tile),),
        in_specs=in_specs,
        out_specs=row,
        out_shape=jax.ShapeDtypeStruct((t, d), F32),
        compiler_params=_params("parallel"),
        name="ffn",
    )(*args)


def _proj_kernel(x_ref, nw_ref, *refs):
    n = len(refs) // 2
    xn = _rms(x_ref[...], nw_ref[...]).astype(BF16)
    for w_ref, o_ref in zip(refs[:n], refs[n:]):
        o_ref[...] = _dot(xn, w_ref[...])


def _norm_proj(h, nw, weights):
    t, d = h.shape
    row = pl.BlockSpec((ROW_TILE, d), lambda i: (i, 0))
    return pl.pallas_call(
        _proj_kernel,
        grid=(pl.cdiv(t, ROW_TILE),),
        in_specs=[row, _resident((1, d))] + [_resident(w.shape) for w in weights],
        out_specs=[pl.BlockSpec((ROW_TILE, w.shape[1]), lambda i: (i, 0)) for w in weights],
        out_shape=[jax.ShapeDtypeStruct((t, w.shape[1]), F32) for w in weights],
        compiler_params=_params("parallel"),
        name="norm_proj",
    )(h, nw.reshape(1, d), *weights)


CONV_HALO = 8


def _proj_conv_kernel(first_ref, xm_ref, xp_ref, xn_ref, nw_ref, wqkv_ref, wgate_ref, wba_ref, cw_ref,
                      qkv_ref, gate_ref, ba_ref, buf_ref, *, tile):
    i = pl.program_id(0)
    last = pl.num_programs(0) - 1
    width = HEADS * HEAD_DIM
    x_all = jnp.concatenate([jnp.where(i > 0, xp_ref[...], 0.0), xm_ref[...],
                             jnp.where(i < last, xn_ref[...], 0.0)], axis=0)
    xn_all = _rms(x_all, nw_ref[...]).astype(BF16)
    mid = slice(CONV_HALO, CONV_HALO + tile)
    half = (CONV_K - 1) // 2
    row = lax.broadcasted_iota(jnp.int32, (CHUNK, 1), 0)
    per_tile = tile // CHUNK

    def project(g):
        buf_ref[g] = _dot(xn_all, wqkv_ref[:, g * width:(g + 1) * width])

    def conv(g):
        cols = slice(g * width, (g + 1) * width)
        taps = [cw_ref[k:k + 1, cols] for k in range(CONV_K)]
        for c in range(per_tile):
            acc = None
            for k in range(CONV_K):
                lo = CONV_HALO - half + k + c * CHUNK
                term = taps[k] * buf_ref[g, lo:lo + CHUNK, :]
                acc = term if acc is None else acc + term
            n_dead = jnp.where(first_ref[i * per_tile + c] > 0, PAD, 0)
            y = jnp.where(row < n_dead, 0.0, acc)
            y = _silu(y)
            for hd in range(HEADS):
                sl = slice(hd * HEAD_DIM, (hd + 1) * HEAD_DIM)
                blk = y[:, sl]
                if g < 2:
                    inv = lax.rsqrt(jnp.sum(blk * blk, axis=-1, keepdims=True) + EPS)
                    blk = blk * (inv * (HEAD_DIM ** -0.5) if g == 0 else inv)
                qkv_ref[c * CHUNK:(c + 1) * CHUNK, g * width + hd * HEAD_DIM:g * width + (hd + 1) * HEAD_DIM] = (
                    blk.astype(qkv_ref.dtype))

    project(0)
    project(1)
    conv(0)
    project(2)
    conv(1)
    gate_ref[...] = _dot(xn_all, wgate_ref[...])[mid].astype(gate_ref.dtype)
    ba_ref[...] = _dot(xn_all, wba_ref[...])[mid]
    conv(2)


def _norm_proj_conv(h, nw, w_in, conv_w, first):
    t, d = h.shape
    width = HEADS * HEAD_DIM
    n_ba = w_in.shape[1] - 4 * width
    tile = CONV_TILE if t % CONV_TILE == 0 else CHUNK
    n_halo = t // CONV_HALO
    tile_halo = tile // CONV_HALO

    def rows(n):
        return pl.BlockSpec((tile, n), lambda i, f: (i, 0))

    def const(shape):
        return pl.BlockSpec(shape, lambda i, f: (0,) * len(shape), pipeline_mode=pl.Buffered(1))

    grid_spec = pltpu.PrefetchScalarGridSpec(
        num_scalar_prefetch=1,
        grid=(t // tile,),
        in_specs=[
            rows(d),
            pl.BlockSpec((CONV_HALO, d), lambda i, f: (jnp.maximum(i * tile_halo - 1, 0), 0)),
            pl.BlockSpec((CONV_HALO, d), lambda i, f: (jnp.minimum((i + 1) * tile_halo, n_halo - 1), 0)),
            const((1, d)), const((d, 3 * width)), const((d, width)), const((d, n_ba)),
            const((CONV_K, 3 * width)),
        ],
        out_specs=[rows(3 * width), rows(width), rows(n_ba)],
        scratch_shapes=[pltpu.VMEM((3, tile + 2 * CONV_HALO, width), F32)],
    )
    return pl.pallas_call(
        functools.partial(_proj_conv_kernel, tile=tile),
        grid_spec=grid_spec,
        out_shape=[jax.ShapeDtypeStruct((t, 3 * width), BF16), jax.ShapeDtypeStruct((t, width), BF16),
                   jax.ShapeDtypeStruct((t, n_ba), F32)],
        compiler_params=_params("arbitrary"),
        name="norm_proj_conv",
    )(first, h, h, h, nw.reshape(1, d), w_in[:, :3 * width], w_in[:, 3 * width:4 * width],
      w_in[:, 4 * width:], conv_w)


def _scan_chunk(d, s, nc):
    return s if d == 0 else nc - 1 - s


def _order_masks(d):
    ii = lax.broadcasted_iota(jnp.int32, (CHUNK, CHUNK), 0)
    jj = lax.broadcasted_iota(jnp.int32, (CHUNK, CHUNK), 1)
    return (ii >= jj, ii > jj) if d == 0 else (ii <= jj, ii < jj)


def _delta_kernel(first_ref, last_ref, qf_ref, kf_ref, vf_ref, qb_ref, kb_ref, vb_ref,
                  bacf_ref, bacb_ref, barf_ref, barb_ref,
                  alog_r_ref, dtb_r_ref, alog_c_ref, dtb_c_ref, of_ref, ob_ref, s_ref, *, cps):
    s = pl.program_id(0)
    nb = pl.num_programs(0)
    block = (s, nb - 1 - s)

    @pl.when(s == 0)
    def _():
        s_ref[...] = jnp.zeros_like(s_ref)

    ii = lax.broadcasted_iota(jnp.int32, (CHUNK, CHUNK), 0)
    jj = lax.broadcasted_iota(jnp.int32, (CHUNK, CHUNK), 1)
    eye = jnp.where(ii == jj, 1.0, 0.0)
    n_lvl = int(math.log2(CHUNK))
    lvl = [((ii >> (b + 1)) == (jj >> (b + 1))) & ((ii >> b) != (jj >> b)) for b in range(n_lvl)]

    q_refs, k_refs, v_refs = (qf_ref, qb_ref), (kf_ref, kb_ref), (vf_ref, vb_ref)
    bac_refs, bar_refs, o_refs = (bacf_ref, bacb_ref), (barf_ref, barb_ref), (of_ref, ob_ref)
    incl, strict, m_incl3, m_incl = [], [], [], []
    for d in range(2):
        inc, strc = _order_masks(d)
        incl.append(inc)
        strict.append(strc)
        m_incl3.append(_order_mask3(d))
        m_incl.append(jnp.where(inc, 1.0, 0.0).astype(BF16))
    rows = [slice(c * CHUNK, (c + 1) * CHUNK) for c in range(cps)]
    beta_c, gc_c, gc_r, gt_c = {}, {}, {}, {}
    for d in range(2):
        for c in range(cps):
            bac = bac_refs[d][rows[c], :]
            bar = bar_refs[d][c]
            la_c = -jnp.exp(alog_r_ref[d]) * _softplus(bac[:, HEADS:] + dtb_r_ref[d])
            la_r = -jnp.exp(alog_c_ref[d]) * _softplus(bar[HEADS:, :] + dtb_c_ref[d])
            beta_c[d, c] = jax.nn.sigmoid(bac[:, :HEADS])
            gc_c[d, c] = _mask_dot(m_incl3[d], la_c)
            gc_r[d, c] = _dot_mask_nt(la_r, m_incl[d])
            gt_c[d, c] = jnp.sum(la_c, axis=0, keepdims=True)

    units = [(d, c, hd) for d in range(2) for c in range(cps) for hd in range(HEADS)]
    n = range(len(units))
    sls = [slice(hd * HEAD_DIM, (hd + 1) * HEAD_DIM) for _, _, hd in units]
    qs = [q_refs[d][rows[c], sls[x]] for x, (d, c, _) in enumerate(units)]
    ks = [k_refs[d][rows[c], sls[x]] for x, (d, c, _) in enumerate(units)]
    vs = [v_refs[d][rows[c], sls[x]] for x, (d, c, _) in enumerate(units)]
    g_col = [gc_c[d, c][:, hd:hd + 1] for d, c, hd in units]
    g_tot = [gt_c[d, c][:, hd:hd + 1] for d, c, hd in units]
    b_col = [beta_c[d, c][:, hd:hd + 1] for d, c, hd in units]
    decay = [jnp.exp(jnp.where(incl[d], g_col[x] - gc_r[d, c][hd:hd + 1, :], -jnp.inf))
             for x, (d, c, hd) in enumerate(units)]
    kb = [ks[x] * b_col[x] for x in n]
    k16 = [x.astype(BF16) for x in ks]
    a_mat = [jnp.where(strict[units[x][0]], _dot_nt(kb[x].astype(BF16), k16[x]) * decay[x], 0.0) for x in n]
    t_inv = [eye - jnp.where(lvl[0], a, 0.0) for a in a_mat]
    for b in range(1, n_lvl):
        t16 = [t.astype(BF16) for t in t_inv]
        a_off = [jnp.where(lvl[b], a, 0.0).astype(BF16) for a in a_mat]
        x16 = [_dot(a_off[x], t16[x]).astype(BF16) for x in n]
        t_inv = [t_inv[x] - _dot(t16[x], x16[x]) for x in n]
    t16 = [t.astype(BF16) for t in t_inv]
    e_col = [jnp.exp(g) for g in g_col]
    uw = [_dot(t16[x], jnp.concatenate([(vs[x] * b_col[x]).astype(BF16),
                                        (kb[x] * e_col[x]).astype(BF16)], axis=1)) for x in n]
    u = [y[:, :HEAD_DIM] for y in uw]
    w16 = [y[:, HEAD_DIM:].astype(BF16) for y in uw]
    qk16 = [(_dot_nt(qs[x].astype(BF16), k16[x]) * decay[x]).astype(BF16) for x in n]
    qd16 = [(qs[x] * e_col[x]).astype(BF16) for x in n]
    kd16 = [(ks[x] * jnp.exp(g_tot[x] - g_col[x])).astype(BF16) for x in n]
    a_last = [jnp.exp(g) for g in g_tot]

    index = {unit: x for x, unit in enumerate(units)}
    state = {(d, hd): s_ref[d, hd] for d in range(2) for hd in range(HEADS)}
    for t in range(cps):
        slot = (t, cps - 1 - t)
        fresh = (first_ref[block[0] * cps + slot[0]] > 0, last_ref[block[1] * cps + slot[1]] > 0)
        cur = [(d, slot[d], hd) for d in range(2) for hd in range(HEADS)]
        idx = [index[k] for k in cur]
        st = [jnp.where(fresh[d], 0.0, state[d, hd]) for d, _, hd in cur]
        st16 = [x.astype(BF16) for x in st]
        ws = [_dot(jnp.concatenate([w16[idx[y]], qd16[idx[y]]], axis=0), st16[y]) for y in range(len(cur))]
        vn16 = [(u[idx[y]] - ws[y][:CHUNK]).astype(BF16) for y in range(len(cur))]
        for y, (d, c, _) in enumerate(cur):
            o_refs[d][rows[c], sls[idx[y]]] = (ws[y][CHUNK:]
                                               + _dot(qk16[idx[y]], vn16[y])).astype(o_refs[d].dtype)
        for y, (d, _, hd) in enumerate(cur):
            state[d, hd] = st[y] * a_last[idx[y]] + _dot_tn(kd16[idx[y]], vn16[y])
    for (d, hd), val in state.items():
        s_ref[d, hd] = val


def _delta_rule(qkv, ba_c, ba_r, a_log, dt_bias, first, last):
    t = qkv.shape[0]
    nc = t // CHUNK
    cps = SCAN_CHUNKS_PER_STEP if nc % SCAN_CHUNKS_PER_STEP == 0 else 1
    nb = nc // cps
    width = HEADS * HEAD_DIM

    def col(j, d):
        return pl.BlockSpec((cps * CHUNK, width), lambda s, f, l: (_scan_chunk(d, s, nb), j))

    def bac(d):
        return pl.BlockSpec((None, cps * CHUNK, 2 * HEADS), lambda s, f, l: (d, _scan_chunk(d, s, nb), 0))

    def bar(d):
        return pl.BlockSpec((None, cps, 2 * HEADS, CHUNK), lambda s, f, l: (d, _scan_chunk(d, s, nb), 0, 0))

    par_r = pl.BlockSpec((2, 1, HEADS), lambda s, f, l: (0, 0, 0))
    par_c = pl.BlockSpec((2, HEADS, 1), lambda s, f, l: (0, 0, 0))
    grid_spec = pltpu.PrefetchScalarGridSpec(
        num_scalar_prefetch=2,
        grid=(nb,),
        in_specs=[col(0, 0), col(1, 0), col(2, 0), col(0, 1), col(1, 1), col(2, 1),
                  bac(0), bac(1), bar(0), bar(1), par_r, par_r, par_c, par_c],
        out_specs=[col(0, 0), col(0, 1)],
        scratch_shapes=[pltpu.VMEM((2, HEADS, HEAD_DIM, HEAD_DIM), F32)],
    )
    return pl.pallas_call(
        functools.partial(_delta_kernel, cps=cps),
        grid_spec=grid_spec,
        out_shape=[jax.ShapeDtypeStruct((t, width), BF16)] * 2,
        compiler_params=_params("arbitrary"),
        name="delta_rule",
    )(first, last, qkv, qkv, qkv, qkv, qkv, qkv, ba_c, ba_c, ba_r, ba_r,
      a_log.reshape(2, 1, HEADS), dt_bias.reshape(2, 1, HEADS),
      a_log.reshape(2, HEADS, 1), dt_bias.reshape(2, HEADS, 1))


def _gated_deltanet(h, nw, w_in, conv_w, a_log, dt_bias, o_norm, w_out, first, last):
    t = h.shape[0]
    width = HEADS * HEAD_DIM
    qkv, gate, ba = _norm_proj_conv(h, nw, w_in, conv_w, first)
    ba = ba.reshape(t, 2, 2, HEADS)
    ba_c = jnp.transpose(ba, (2, 0, 1, 3)).reshape(2, t, 2 * HEADS)
    ba_r = jnp.transpose(ba_c.reshape(2, t // CHUNK, CHUNK, 2 * HEADS), (0, 1, 3, 2))
    o_f, o_b = _delta_rule(qkv, ba_c, ba_r, a_log, dt_bias, first, last)
    return ([o_f, o_b], gate, 0, o_norm, w_out, 1.0)


def _gla_level_matrices():
    ii = np.arange(CHUNK)[:, None]
    jj = np.arange(CHUNK)[None, :]
    out = []
    for d in range(2):
        mats = [(ii >= jj) if d == 0 else (ii <= jj)]
        for b in range(int(math.log2(CHUNK))):
            size = 1 << b
            mid = ((ii >> (b + 1)) << (b + 1)) + size
            upper = (ii & size) != 0
            if d == 0:
                sel = (upper & (jj >= mid) & (jj <= ii)) | (~upper & (jj > ii) & (jj < mid))
            else:
                sel = (upper & (jj >= mid) & (jj < ii)) | (~upper & (jj >= ii) & (jj < mid))
            mats.append(sel)
        out.append(np.tile(np.concatenate(mats, axis=0), (1, 3)))
    return np.stack(out).astype(np.float32)


def _gla_kernel(first_ref, last_ref, qf_ref, vf_ref, ff_ref, qb_ref, vb_ref, fb_ref, lb_ref, lvl_ref,
                of_ref, ob_ref, s_ref, *, layer_idx, cps):
    s = pl.program_id(0)
    nb = pl.num_programs(0)
    block = (s, nb - 1 - s)

    @pl.when(s == 0)
    def _():
        s_ref[...] = jnp.zeros_like(s_ref)

    ii = lax.broadcasted_iota(jnp.int32, (CHUNK, CHUNK), 0)
    jj = lax.broadcasted_iota(jnp.int32, (CHUNK, CHUNK), 1)
    n_lvl = int(math.log2(CHUNK))

    lbl = lb_ref[...]
    lbw = jnp.exp(lbl - jnp.max(lbl, axis=0, keepdims=True))
    lbw = lbw / jnp.sum(lbw, axis=0, keepdims=True)
    lb = jnp.zeros_like(lbw[0:1, :])
    for r in range(1, layer_idx + 1):
        lb = lb + lbw[r:r + 1, :]

    q_refs, v_refs, f_refs, o_refs = (qf_ref, qb_ref), (vf_ref, vb_ref), (ff_ref, fb_ref), (of_ref, ob_ref)
    rows = [slice(c * CHUNK, (c + 1) * CHUNK) for c in range(cps)]
    pair_masks = []
    for d in range(2):
        _, strict = _order_masks(d)
        pair_masks.append([((ii >> (b + 1)) == (jj >> (b + 1))) & ((ii >> b) != (jj >> b)) & strict
                           for b in range(n_lvl)])
    q, kk, x_all, q_dec, k_dec, f_last = {}, {}, {}, {}, {}, {}
    for d in range(2):
        for c in range(cps):
            f = lb + (1.0 - lb) * jax.nn.sigmoid(f_refs[d][rows[c], :])
            logf = jnp.maximum(jnp.log(f), LOG_GATE_FLOOR)
            e_all = _mask_dot(lvl_ref[d], logf)
            tot = jnp.sum(logf, axis=0, keepdims=True)
            xa = jnp.exp(e_all)
            q[d, c] = _silu(q_refs[d][rows[c], :]) * (HEAD_DIM ** -0.5)
            kk[d, c] = 1.0 - f
            x_all[d, c] = xa.astype(BF16)
            q_dec[d, c] = q[d, c] * xa[:CHUNK, :]
            k_dec[d, c] = (1.0 - f) * jnp.exp(tot - e_all[:CHUNK, :])
            f_last[d, c] = jnp.exp(tot)

    units = [(d, c, hd) for d in range(2) for c in range(cps) for hd in range(HEADS)]
    n = range(len(units))
    sls = [slice(hd * HEAD_DIM, (hd + 1) * HEAD_DIM) for _, _, hd in units]
    q16 = [q[d, c][:, sls[x]].astype(BF16) for x, (d, c, _) in enumerate(units)]
    k16 = [kk[d, c][:, sls[x]].astype(BF16) for x, (d, c, _) in enumerate(units)]
    attn = [jnp.where(ii == jj, _dot_nt(q16[x], k16[x]), 0.0) for x in n]
    for b in range(n_lvl):
        xb = [x_all[d, c][(b + 1) * CHUNK:(b + 2) * CHUNK, sls[x]] for x, (d, c, _) in enumerate(units)]
        part = [_dot_nt(q16[x] * xb[x], k16[x] * xb[x]) for x in n]
        attn = [jnp.where(pair_masks[units[x][0]][b], part[x], attn[x]) for x in n]
    attn16 = [a.astype(BF16) for a in attn]
    v16 = [v_refs[d][rows[c], sls[x]].astype(BF16) for x, (d, c, _) in enumerate(units)]
    qd16 = [q_dec[d, c][:, sls[x]].astype(BF16) for x, (d, c, _) in enumerate(units)]
    kd16 = [k_dec[d, c][:, sls[x]].astype(BF16) for x, (d, c, _) in enumerate(units)]

    index = {unit: x for x, unit in enumerate(units)}
    state = {(d, hd): s_ref[d, hd] for d in range(2) for hd in range(HEADS)}
    for t in range(cps):
        slot = (t, cps - 1 - t)
        fresh = (first_ref[block[0] * cps + slot[0]] > 0, last_ref[block[1] * cps + slot[1]] > 0)
        cur = [(d, slot[d], hd) for d in range(2) for hd in range(HEADS)]
        idx = [index[k] for k in cur]
        st = [jnp.where(fresh[d], 0.0, state[d, hd]) for d, _, hd in cur]
        for y, (d, c, _) in enumerate(cur):
            o_refs[d][rows[c], sls[idx[y]]] = (_dot_nt(qd16[idx[y]], st[y].astype(BF16))
                                               + _dot(attn16[idx[y]], v16[idx[y]])).astype(o_refs[d].dtype)
        for y, (d, c, hd) in enumerate(cur):
            state[d, hd] = st[y] * f_last[d, c][:, sls[idx[y]]] + _dot_tn(v16[idx[y]], kd16[idx[y]])
    for (d, hd), val in state.items():
        s_ref[d, hd] = val


def _gla(proj, lb_logits, layer_idx, first, last):
    t = proj.shape[0]
    nc = t // CHUNK
    cps = SCAN_CHUNKS_PER_STEP if nc % SCAN_CHUNKS_PER_STEP == 0 else 1
    nb = nc // cps
    width = HEADS * HEAD_DIM

    lvl = jnp.asarray(_gla_level_matrices(), BF16)

    def col(j, d):
        return pl.BlockSpec((cps * CHUNK, width), lambda s, f, l: (_scan_chunk(d, s, nb), j))

    grid_spec = pltpu.PrefetchScalarGridSpec(
        num_scalar_prefetch=2,
        grid=(nb,),
        in_specs=[col(0, 0), col(1, 0), col(3, 0), col(0, 1), col(1, 1), col(4, 1),
                  pl.BlockSpec(lb_logits.shape, lambda s, f, l: (0, 0)),
                  pl.BlockSpec(lvl.shape, lambda s, f, l: (0, 0, 0))],
        out_specs=[col(0, 0), col(0, 1)],
        scratch_shapes=[pltpu.VMEM((2, HEADS, HEAD_DIM, HEAD_DIM), F32)],
    )
    return pl.pallas_call(
        functools.partial(_gla_kernel, layer_idx=layer_idx, cps=cps),
        grid_spec=grid_spec,
        out_shape=[jax.ShapeDtypeStruct((t, width), BF16)] * 2,
        compiler_params=_params("arbitrary"),
        name="gla",
    )(first, last, proj, proj, proj, proj, proj, proj, lb_logits, lvl)


def _hgrn2(h, nw, w_in, lb_logits, layer_idx, o_norm, w_out, first, last):
    (proj,) = _norm_proj(h, nw, [w_in])
    o_f, o_b = _gla(proj, lb_logits, layer_idx, first, last)
    return ([o_f, o_b], proj, 2, o_norm, w_out, 1.0)


def _proj_rope_kernel(x_ref, nw_ref, w_ref, cos_ref, s1_ref, s2_ref, o_ref, *, n_rot_tiles, n_q_tiles):
    xn = _rms(x_ref[...], nw_ref[...]).astype(BF16)
    cos = cos_ref[...]
    s1 = s1_ref[...]
    s2 = s2_ref[...]
    half = ATT_ROT // 2
    for c2 in range(0, n_rot_tiles, 2):
        wide = _dot(xn, w_ref[:, c2 * LANES:(c2 + 2) * LANES])
        for c in (c2, c2 + 1):
            sl = slice(c * LANES, (c + 1) * LANES)
            x = wide[:, (c - c2) * LANES:(c - c2 + 1) * LANES]
            y = x * cos + pltpu.roll(x, LANES - half, 1) * s1 + pltpu.roll(x, half, 1) * s2
            if c < n_q_tiles:
                y = y * (ATT_HD ** -0.5 * LOG2E)
            o_ref[:, sl] = y.astype(BF16)
    rest = slice(n_rot_tiles * LANES, w_ref.shape[1])
    o_ref[:, rest] = _dot(xn, w_ref[:, rest]).astype(BF16)


def _norm_proj_rope(h, nw, w_in, cos_t, s1_t, s2_t):
    t, d = h.shape
    n = w_in.shape[1]
    qk_w = HEADS * ATT_HD
    row = pl.BlockSpec((ROW_TILE, d), lambda i: (i, 0))
    tab = pl.BlockSpec((ROW_TILE, LANES), lambda i: (i, 0))
    return pl.pallas_call(
        functools.partial(_proj_rope_kernel, n_rot_tiles=4 * qk_w // LANES, n_q_tiles=2 * qk_w // LANES),
        grid=(pl.cdiv(t, ROW_TILE),),
        in_specs=[row, _resident((1, d)), _resident(w_in.shape), tab, tab, tab],
        out_specs=pl.BlockSpec((ROW_TILE, n), lambda i: (i, 0)),
        out_shape=jax.ShapeDtypeStruct((t, n), BF16),
        compiler_params=_params("parallel"),
        name="norm_proj_rope",
    )(h, nw.reshape(1, d), w_in, cos_t, s1_t, s2_t)


def _key_tiles(seq_len):
    n = max(1, seq_len // ATT_TK)
    return [ATT_TK] * (n - 1) + [seq_len - (n - 1) * ATT_TK]


def _attn_kernel(q1_ref, q2_ref, k1_ref, k2_ref, v_ref, lam_ref, o_ref, vt_ref, *scratch,
                 seq_len, lam_init):
    i = pl.program_id(2)
    sizes = _key_tiles(seq_len)
    n_tiles = len(sizes)
    acc_refs, sc_refs = scratch[:4], scratch[4:]

    @pl.when(i == 0)
    def _():
        def body(j, carry):
            blk = v_ref[pl.ds(pl.multiple_of(j * ATT_TK, ATT_TK), ATT_TK), :]
            vt_ref[j, :, :ATT_TK] = blk.astype(F32).T.astype(BF16)
            return carry
        if n_tiles > 1:
            lax.fori_loop(0, n_tiles - 1, body, 0)
        blk = v_ref[(n_tiles - 1) * ATT_TK:seq_len, :]
        vt_ref[n_tiles - 1, :, :sizes[-1]] = blk.astype(F32).T.astype(BF16)

    lane = lax.broadcasted_iota(jnp.int32, (1, LANES), 1)
    q_refs = (q1_ref, q2_ref)
    k_refs = (k1_ref, k2_ref)
    qm = []
    for hh in range(2):
        keep = (lane >= hh * ATT_HD) & (lane < (hh + 1) * ATT_HD)
        qm.append([jnp.where(keep, q_refs[m][...], jnp.zeros((), BF16)) for m in range(2)])

    for a in acc_refs:
        a[...] = jnp.zeros_like(a)
    units = [(hh, m) for hh in range(2) for m in range(2)]

    def score(u, j, size, masked):
        hh, m = units[u]
        start = j * ATT_TK
        if not isinstance(j, int):
            start = pl.multiple_of(start, ATT_TK)
        sc = _dot_nt(k_refs[m][pl.ds(start, size), :], qm[hh][m])
        if masked:
            rowid = lax.broadcasted_iota(jnp.int32, (size, 1), 0)
            sc = jnp.where(rowid >= PAD, sc, -jnp.inf)
        return sc

    def consume(u, sc, j, size, carry):
        hh = units[u][0]
        vt = vt_ref[j, hh * HEAD_DIM:(hh + 1) * HEAD_DIM, :size]
        m_old, l_old = carry[u], carry[4 + u]
        m_new = jnp.maximum(m_old, jnp.max(sc, axis=0, keepdims=True))
        alpha = jnp.exp2(m_old - m_new)
        p = jnp.exp2(sc - m_new)
        l_new = alpha * l_old + jnp.sum(p, axis=0, keepdims=True)
        acc_refs[u][...] = alpha * acc_refs[u][...] + _dot(vt, p.astype(BF16))
        return m_new, l_new

    bufs = (sc_refs[:4], sc_refs[4:])

    def step(j, size, carry, cur, nxt=None, size_n=None):
        def issue(u):
            nxt[u][:size_n, :] = score(u, j + 1, size_n, False)
        if nxt is not None:
            issue(0)
            issue(1)
        stats = []
        for u in range(4):
            stats.append(consume(u, cur[u][:size, :], j, size, carry))
            if nxt is not None and u + 2 < 4:
                issue(u + 2)
        return tuple(s[0] for s in stats) + tuple(s[1] for s in stats)

    carry = (jnp.full((1, ATT_TQ), -jnp.inf, F32),) * 4 + (jnp.zeros((1, ATT_TQ), F32),) * 4
    for u in range(4):
        bufs[0][u][:sizes[0], :] = score(u, 0, sizes[0], True)
    n_pairs = max(0, (n_tiles - 2) // 2)
    if n_pairs > 0:
        def body(i, c):
            c = step(2 * i, ATT_TK, c, bufs[0], bufs[1], ATT_TK)
            return step(2 * i + 1, ATT_TK, c, bufs[1], bufs[0], ATT_TK)
        carry = lax.fori_loop(0, n_pairs, body, carry)
    for t in range(2 * n_pairs, n_tiles):
        if t + 1 < n_tiles:
            carry = step(t, sizes[t], carry, bufs[t % 2], bufs[(t + 1) % 2], sizes[t + 1])
        else:
            carry = step(t, sizes[t], carry, bufs[t % 2])
    l_fin = carry[4:]

    lv = lam_ref[...]
    lam = (jnp.exp(jnp.sum(lv[0:1] * lv[1:2], axis=-1, keepdims=True))
           - jnp.exp(jnp.sum(lv[2:3] * lv[3:4], axis=-1, keepdims=True)) + lam_init)
    row = i * ATT_TQ + lax.broadcasted_iota(jnp.int32, (ATT_TQ, 1), 0)
    for hh in range(2):
        o_t = (acc_refs[2 * hh][...] / l_fin[2 * hh]
               - lam * (acc_refs[2 * hh + 1][...] / l_fin[2 * hh + 1]))
        o_ref[:, hh * HEAD_DIM:(hh + 1) * HEAD_DIM] = jnp.where(row >= PAD, o_t.T, 0.0).astype(o_ref.dtype)


def _attention(qkv, lam_vec, lam_init):
    b, seq_len, _ = qkv.shape
    width = HEADS * HEAD_DIM
    n_pair = HEADS // 2
    pair_w = 2 * ATT_HD
    nq = pl.cdiv(seq_len, ATT_TQ)
    sizes = _key_tiles(seq_len)

    def qspec(off):
        return pl.BlockSpec((None, ATT_TQ, pair_w), lambda bb, p, i: (bb, i, off + p))

    def kspec(off):
        return pl.BlockSpec((None, seq_len, pair_w), lambda bb, p, i: (bb, 0, off + p))

    return pl.pallas_call(
        functools.partial(_attn_kernel, seq_len=seq_len, lam_init=lam_init),
        grid=(b, n_pair, nq),
        in_specs=[
            qspec(0), qspec(n_pair), kspec(2 * n_pair), kspec(3 * n_pair),
            pl.BlockSpec((None, seq_len, 2 * HEAD_DIM), lambda bb, p, i: (bb, 0, 2 * n_pair + p)),
            pl.BlockSpec(lam_vec.shape, lambda bb, p, i: (0, 0)),
        ],
        out_specs=pl.BlockSpec((None, ATT_TQ, 2 * HEAD_DIM), lambda bb, p, i: (bb, i, p)),
        out_shape=jax.ShapeDtypeStruct((b, seq_len, width), BF16),
        scratch_shapes=[pltpu.VMEM((len(sizes), 2 * HEAD_DIM, max(sizes)), BF16)]
        + [pltpu.VMEM((HEAD_DIM, ATT_TQ), F32)] * 4
        + [pltpu.VMEM((max(sizes), ATT_TQ), F32)] * 8,
        compiler_params=_params("arbitrary", "arbitrary", "arbitrary"),
        name="diff_attention",
    )(qkv, qkv, qkv, qkv, qkv, lam_vec)


def _rope_tables(pos):
    half = ATT_ROT // 2
    inv_freq = jnp.exp(-math.log(ROPE_THETA) * jnp.arange(half, dtype=F32) / half)
    ang = pos.astype(F32)[:, None] * inv_freq
    cos, sin = jnp.cos(ang), jnp.sin(ang)
    n = pos.shape[0]
    rest = ATT_HD - ATT_ROT
    cos_t = jnp.concatenate([cos, cos, jnp.ones((n, rest), F32)], axis=1)
    s1_t = jnp.concatenate([-sin, jnp.zeros((n, ATT_HD - half), F32)], axis=1)
    s2_t = jnp.concatenate([jnp.zeros((n, half), F32), sin, jnp.zeros((n, rest), F32)], axis=1)
    rep = LANES // ATT_HD
    return tuple(jnp.tile(x, (1, rep)) for x in (cos_t, s1_t, s2_t))


def _diff_attention(h, nw, w_in, lam_vec, sub_norm, w_out, layer_idx, groups, pos):
    roped = _norm_proj_rope(h, nw, w_in, *_rope_tables(pos))
    lam_init = 0.8 - 0.6 * math.exp(-0.3 * layer_idx)
    outs = []
    start = 0
    for b, seq_len in groups:
        rows = b * seq_len
        part = roped[start:start + rows].reshape(b, seq_len, roped.shape[1])
        outs.append(_attention(part, lam_vec, lam_init).reshape(rows, -1))
        start += rows
    o = jnp.concatenate(outs, axis=0)
    return ([o], None, 0, sub_norm, w_out, 1.0 - lam_init)


def kernel(x_prompt, x_sample, meta_tokens, norm_w, ffn_w_up, ffn_w_down,
           a_w_in, a_conv_w, a_log, a_dt_bias, a_o_norm, a_w_out,
           b_w_in, b_lambda, b_sub_norm, b_w_out,
           c_w_in, c_lb_logits, c_o_norm, c_w_out, final_norm):
    d = x_prompt.shape[-1]
    depth = norm_w.shape[0]
    xs = (x_prompt, x_sample)
    groups = [(x.shape[0], x.shape[1] + CHUNK) for x in xs]

    lead = jnp.concatenate([jnp.zeros((PAD, d), F32), meta_tokens.astype(F32)], axis=0)
    parts = []
    for x in xs:
        for b in range(x.shape[0]):
            parts += [lead, x[b].astype(F32)]
    h = jnp.concatenate(parts, axis=0)
    t = h.shape[0]

    first = np.zeros((t // CHUNK,), np.int32)
    last = np.zeros((t // CHUNK,), np.int32)
    pos = np.zeros((t,), np.int32)
    start = 0
    for b, seq_len in groups:
        for _ in range(b):
            first[start // CHUNK] = 1
            last[(start + seq_len) // CHUNK - 1] = 1
            pos[start:start + seq_len] = np.maximum(np.arange(seq_len) - PAD, 0)
            start += seq_len
    first, last, pos = jnp.asarray(first), jnp.asarray(last), jnp.asarray(pos)

    w_up16 = ffn_w_up.astype(BF16)
    w_dn16 = ffn_w_down.astype(BF16)
    for i in range(depth):
        kind, j = i % N_MIXERS, i // N_MIXERS
        h = _ffn(h, norm_w[i, 0], w_up16[i, 0], w_dn16[i, 0], final_norm, False)
        if kind == 0:
            mix = _gated_deltanet(h, norm_w[i, 1], a_w_in[j].astype(BF16), a_conv_w[j], a_log[j],
                                  a_dt_bias[j], a_o_norm[j], a_w_out[j].astype(BF16), first, last)
        elif kind == 1:
            mix = _diff_attention(h, norm_w[i, 1], b_w_in[j].astype(BF16), b_lambda[j], b_sub_norm[j],
                                  b_w_out[j].astype(BF16), i, groups, pos)
        else:
            mix = _hgrn2(h, norm_w[i, 1], c_w_in[j].astype(BF16), c_lb_logits, i, c_o_norm[j],
                         c_w_out[j].astype(BF16), first, last)
        h = _ffn(h, norm_w[i, 2], w_up16[i, 1], w_dn16[i, 1], final_norm, i == depth - 1, mix=mix)

    h3 = h.reshape(t // CHUNK, CHUNK, d)
    outs = []
    start = 0
    for (b, seq_len), x in zip(groups, xs):
        per_seq = seq_len // CHUNK
        pieces = [h3[start + s * per_seq + 1:start + (s + 1) * per_seq] for s in range(b)]
        outs.append(jnp.concatenate(pieces, axis=0).reshape(b, seq_len - CHUNK, d).astype(x.dtype))
        start += b * per_seq
    return tuple(outs)
```
